```python
import jax
import jax.numpy as jnp
from jax import lax
import numpy as np

D_MODEL = 1024
BATCH = 8
SEQ = 2048
DEPTH = 1
DEC_BATCH = 128
DEC_SEQ = 1
PAST_LEN = 8192
PAGE_SIZE = 128

N_PAGES = PAST_LEN // PAGE_SIZE
N_POOL = (DEC_BATCH * N_PAGES * 5) // 4

MOBA_HEADS = 8
MOBA_HEAD_DIM = 64
MOBA_WIDTH = MOBA_HEADS * MOBA_HEAD_DIM
MOBA_BLOCK = 256
MOBA_TOPK = 3
MOBA_Q_CHUNK = 128

MLA_HEADS = 8
MLA_NOPE = 64
MLA_ROPE = 32
MLA_QK_DIM = MLA_NOPE + MLA_ROPE
MLA_V_DIM = 64
MLA_WIDTH = MLA_HEADS * MLA_V_DIM
MLA_Q_LORA = 384
MLA_KV_LORA = 256
MLA_Q_BLOCK = 128

IN_WIDTH = 4 * MOBA_WIDTH + MLA_Q_LORA + MLA_KV_LORA + MLA_ROPE + MLA_WIDTH + 2 * D_MODEL
ROPE_THETA = 10000.0
EPS = 1e-6
NEG = -1e30

kernel_name = 'moba_mla_parallel_gated_decode_step'


def rms_norm(x, g):
    xf = x.astype(jnp.float32)
    y = xf * lax.rsqrt(jnp.mean(xf * xf, axis=-1, keepdims=True) + EPS)
    return (y * g.astype(jnp.float32)).astype(x.dtype)


def rope(x, pos):
    half = x.shape[-1] // 2
    inv = jnp.power(ROPE_THETA, -jnp.arange(half, dtype=jnp.float32) / half)
    ang = pos.astype(jnp.float32)[:, None] * inv[None, :]
    cos = jnp.cos(ang)[:, None, :]
    sin = jnp.sin(ang)[:, None, :]
    xf = x.astype(jnp.float32)
    x1, x2 = xf[..., :half], xf[..., half:]
    return jnp.concatenate([x1 * cos - x2 * sin, x2 * cos + x1 * sin], axis=-1).astype(x.dtype)


def split_in(z):
    sizes = (MOBA_WIDTH, MOBA_WIDTH, MOBA_WIDTH, MOBA_WIDTH, MLA_Q_LORA, MLA_KV_LORA, MLA_ROPE, MLA_WIDTH, D_MODEL, D_MODEL)
    offs = []
    acc = 0
    for s in sizes[:-1]:
        acc += s
        offs.append(acc)
    return jnp.split(z, offs, axis=-1)


def front(x, pos, norm_gain, w_in, moba_q_gain, moba_k_gain, mla_q_lat_gain, w_uq, mla_q_gain, mla_kv_lat_gain):
    b, t, _ = x.shape
    h = rms_norm(x, norm_gain)
    z = jnp.einsum('btd,de->bte', h, w_in)
    qa, ka, va, ga, cq, ckv, kpe, gb, g1, g2 = split_in(z)
    qa = rope(rms_norm(qa.reshape(b, t, MOBA_HEADS, MOBA_HEAD_DIM), moba_q_gain), pos)
    ka = rope(rms_norm(ka.reshape(b, t, MOBA_HEADS, MOBA_HEAD_DIM), moba_k_gain), pos)
    va = va.reshape(b, t, MOBA_HEADS, MOBA_HEAD_DIM)
    qm = jnp.einsum('btr,rhe->bthe', rms_norm(cq, mla_q_lat_gain), w_uq)
    qm = rms_norm(qm, mla_q_gain)
    qm = jnp.concatenate([qm[..., :MLA_NOPE], rope(qm[..., MLA_NOPE:], pos)], axis=-1)
    ckv = rms_norm(ckv, mla_kv_lat_gain)
    return qa, ka, va, ga, qm, ckv, kpe, gb, g1, g2


def moba_attend(q, qpos, kh, vh, k_means):
    c = q.shape[0]
    nb = kh.shape[1]
    topk = min(MOBA_TOPK, nb)
    cur = qpos // MOBA_BLOCK
    gate = jnp.einsum('chd,hnd->chn', q.astype(jnp.float32), k_means)
    fully_past = jnp.arange(nb, dtype=jnp.int32)[None, None, :] < cur[:, None, None]
    gate = jnp.where(fully_past, gate, NEG)
    _, top_idx = lax.top_k(gate, topk)
    sel_ok = top_idx < cur[:, None, None]
    blk = jnp.concatenate([top_idx, jnp.broadcast_to(cur[:, None, None], (c, MOBA_HEADS, 1))], axis=-1)
    hidx = jnp.arange(MOBA_HEADS)[None, :, None]
    kg = kh[hidx, blk]
    vg = vh[hidx, blk]
    kpos = blk[..., None] * MOBA_BLOCK + jnp.arange(MOBA_BLOCK, dtype=jnp.int32)
    ok = jnp.concatenate([jnp.broadcast_to(sel_ok[..., None], (c, MOBA_HEADS, topk, MOBA_BLOCK)),
                          kpos[:, :, topk:, :] <= qpos[:, None, None, None]], axis=2)
    s = jnp.einsum('chd,chjsd->chjs', q, kg).astype(jnp.float32) * (MOBA_HEAD_DIM ** -0.5)
    s = jnp.where(ok, s, NEG)
    p = jax.nn.softmax(s.reshape(c, MOBA_HEADS, -1), axis=-1).reshape(s.shape).astype(vg.dtype)
    return jnp.einsum('chjs,chjsd->chd', p, vg)


def moba_sequence(q, qpos, k, v, q_chunk):
    t = q.shape[0]
    n_keys = k.shape[0]
    nb = -(-n_keys // MOBA_BLOCK)
    pad = nb * MOBA_BLOCK - n_keys
    kh = jnp.pad(k, ((0, pad), (0, 0), (0, 0))).reshape(nb, MOBA_BLOCK, MOBA_HEADS, MOBA_HEAD_DIM).transpose(2, 0, 1, 3)
    vh = jnp.pad(v, ((0, pad), (0, 0), (0, 0))).reshape(nb, MOBA_BLOCK, MOBA_HEADS, MOBA_HEAD_DIM).transpose(2, 0, 1, 3)
    k_means = jnp.mean(kh.astype(jnp.float32), axis=2)
    n_chunks = t // q_chunk
    qc = q.reshape(n_chunks, q_chunk, MOBA_HEADS, MOBA_HEAD_DIM)
    pc = qpos.reshape(n_chunks, q_chunk)
    out = lax.map(lambda a: moba_attend(a[0], a[1], kh, vh, k_means), (qc, pc))
    return out.reshape(t, MOBA_WIDTH)


def mla_kv(ckv, kpe, pos, w_uk, w_uv, mla_k_gain):
    k_nope = jnp.einsum('...lr,rhd->...lhd', ckv, w_uk)
    kpe_h = jnp.broadcast_to(kpe[..., None, :], k_nope.shape[:-1] + (MLA_ROPE,))
    k = rms_norm(jnp.concatenate([k_nope, kpe_h], axis=-1), mla_k_gain)
    k = jnp.concatenate([k[..., :MLA_NOPE], rope(k[..., MLA_NOPE:], pos)], axis=-1)
    v = jnp.einsum('...lr,rhd->...lhd', ckv, w_uv)
    return k, v


def masked_attention(q, k, v, qpos, kpos):
    s = jnp.einsum('...qhd,...khd->...hqk', q, k).astype(jnp.float32) * (q.shape[-1] ** -0.5)
    s = jnp.where(kpos[None, :] <= qpos[:, None], s, NEG)
    p = jax.nn.softmax(s, axis=-1).astype(v.dtype)
    return jnp.einsum('...hqk,...khd->...qhd', p, v)


def merge(x, o_a, ga, o_b, gb, g1, g2, w_branch_a, w_branch_b, w_out):
    a = jnp.einsum('bte,ed->btd', o_a * jax.nn.silu(ga), w_branch_a)
    b = jnp.einsum('bte,ed->btd', o_b * jax.nn.silu(gb), w_branch_b)
    h = jax.nn.sigmoid(g1) * a + jax.nn.sigmoid(g2) * b
    return x + jnp.einsum('btd,de->bte', h, w_out)


def setup_inputs(seed: int = 0) -> dict:
    key = jax.random.key(seed)
    ks = jax.random.split(key, 24)
    f32 = jnp.float32

    def nrm(k, shape, scale):
        return jax.random.normal(k, shape, f32) * scale

    def gain(k, shape):
        return 1.0 + 0.02 * jax.random.normal(k, shape, f32)

    page_table = jax.random.permutation(ks[6], N_POOL)[: DEC_BATCH * N_PAGES].reshape(DEC_BATCH, N_PAGES).astype(jnp.int32)
    return {
        'x_prompt': nrm(ks[0], (BATCH, SEQ, D_MODEL), 1.0),
        'x_sample': nrm(ks[1], (DEC_BATCH, DEC_SEQ, D_MODEL), 1.0),
        'cache_moba_k': nrm(ks[2], (DEPTH, N_POOL, PAGE_SIZE, MOBA_HEADS, MOBA_HEAD_DIM), 1.0),
        'cache_moba_v': nrm(ks[3], (DEPTH, N_POOL, PAGE_SIZE, MOBA_HEADS, MOBA_HEAD_DIM), 1.0),
        'cache_mla_ckv': nrm(ks[4], (DEPTH, N_POOL, PAGE_SIZE, MLA_KV_LORA), 1.0),
        'cache_mla_kpe': nrm(ks[5], (DEPTH, N_POOL, PAGE_SIZE, MLA_ROPE), 1.0),
        'page_table': page_table,
        'norm_gain': gain(ks[7], (DEPTH, D_MODEL)),
        'w_in': nrm(ks[8], (DEPTH, D_MODEL, IN_WIDTH), D_MODEL ** -0.5),
        'moba_q_gain': gain(ks[9], (DEPTH, MOBA_HEAD_DIM)),
        'moba_k_gain': gain(ks[10], (DEPTH, MOBA_HEAD_DIM)),
        'mla_q_lat_gain': gain(ks[11], (DEPTH, MLA_Q_LORA)),
        'w_uq': nrm(ks[12], (DEPTH, MLA_Q_LORA, MLA_HEADS, MLA_QK_DIM), MLA_Q_LORA ** -0.5),
        'mla_q_gain': gain(ks[13], (DEPTH, MLA_QK_DIM)),
        'mla_kv_lat_gain': gain(ks[14], (DEPTH, MLA_KV_LORA)),
        'w_uk': nrm(ks[15], (DEPTH, MLA_KV_LORA, MLA_HEADS, MLA_NOPE), MLA_KV_LORA ** -0.5),
        'w_uv': nrm(ks[16], (DEPTH, MLA_KV_LORA, MLA_HEADS, MLA_V_DIM), MLA_KV_LORA ** -0.5),
        'mla_k_gain': gain(ks[17], (DEPTH, MLA_QK_DIM)),
        'w_branch_a': nrm(ks[18], (DEPTH, MOBA_WIDTH, D_MODEL), MOBA_WIDTH ** -0.5),
        'w_branch_b': nrm(ks[19], (DEPTH, MLA_WIDTH, D_MODEL), MLA_WIDTH ** -0.5),
        'w_out': nrm(ks[20], (DEPTH, D_MODEL, D_MODEL), 0.5 * D_MODEL ** -0.5),
    }


def reference(x_prompt, x_sample, cache_moba_k, cache_moba_v, cache_mla_ckv, cache_mla_kpe, page_table,
              norm_gain, w_in, moba_q_gain, moba_k_gain, mla_q_lat_gain, w_uq, mla_q_gain,
              mla_kv_lat_gain, w_uk, w_uv, mla_k_gain, w_branch_a, w_branch_b, w_out):
    n_b = x_prompt.shape[0]
    pos_p = jnp.arange(SEQ, dtype=jnp.int32)
    pos_s = PAST_LEN + jnp.arange(DEC_SEQ, dtype=jnp.int32)
    pos_all = jnp.arange(PAST_LEN + DEC_SEQ, dtype=jnp.int32)
    n_qb = SEQ // MLA_Q_BLOCK
    xp, xs = x_prompt, x_sample
    nk_p, nv_p, nc_p, np_p = [], [], [], []
    nk_s, nv_s, nc_s, np_s = [], [], [], []
    for l in range(DEPTH):
        fw = (norm_gain[l], w_in[l], moba_q_gain[l], moba_k_gain[l], mla_q_lat_gain[l], w_uq[l], mla_q_gain[l], mla_kv_lat_gain[l])
        qa, ka, va, ga, qm, ckv, kpe, gb, g1, g2 = front(xp, pos_p, *fw)
        o_a = lax.map(lambda a: moba_sequence(a[0], pos_p, a[1], a[2], MOBA_Q_CHUNK), (qa, ka, va))
        km, vm = mla_kv(ckv, kpe, pos_p, w_uk[l], w_uv[l], mla_k_gain[l])
        qb = qm.reshape(n_b, n_qb, MLA_Q_BLOCK, MLA_HEADS, MLA_QK_DIM).transpose(1, 0, 2, 3, 4)
        pb = pos_p.reshape(n_qb, MLA_Q_BLOCK)
        o_b = lax.map(lambda a: masked_attention(a[0], km, vm, a[1], pos_p), (qb, pb))
        o_b = o_b.transpose(1, 0, 2, 3, 4).reshape(n_b, SEQ, MLA_WIDTH)
        xp = merge(xp, o_a, ga, o_b, gb, g1, g2, w_branch_a[l], w_branch_b[l], w_out[l])
        nk_p.append(ka)
        nv_p.append(va)
        nc_p.append(ckv)
        np_p.append(kpe)
        qa_s, ka_s, va_s, ga_s, qm_s, ckv_s, kpe_s, gb_s, g1_s, g2_s = front(xs, pos_s, *fw)

        def decode_one(args, l=l):
            q_a, k_a, v_a, q_m, c_new, p_new, pages = args
            k_all = jnp.concatenate([cache_moba_k[l, pages].reshape(PAST_LEN, MOBA_HEADS, MOBA_HEAD_DIM), k_a], axis=0)
            v_all = jnp.concatenate([cache_moba_v[l, pages].reshape(PAST_LEN, MOBA_HEADS, MOBA_HEAD_DIM), v_a], axis=0)
            oa = moba_sequence(q_a, pos_s, k_all, v_all, DEC_SEQ)
            c_all = jnp.concatenate([cache_mla_ckv[l, pages].reshape(PAST_LEN, MLA_KV_LORA), c_new], axis=0)
            p_all = jnp.concatenate([cache_mla_kpe[l, pages].reshape(PAST_LEN, MLA_ROPE), p_new], axis=0)
            kd, vd = mla_kv(c_all, p_all, pos_all, w_uk[l], w_uv[l], mla_k_gain[l])
            ob = masked_attention(q_m, kd, vd, pos_s, pos_all).reshape(DEC_SEQ, MLA_WIDTH)
            return oa, ob

        o_a_s, o_b_s = lax.map(decode_one, (qa_s, ka_s, va_s, qm_s, ckv_s, kpe_s, page_table))
        xs = merge(xs, o_a_s, ga_s, o_b_s, gb_s, g1_s, g2_s, w_branch_a[l], w_branch_b[l], w_out[l])
        nk_s.append(ka_s)
        nv_s.append(va_s)
        nc_s.append(ckv_s)
        np_s.append(kpe_s)
    return (xp, xs,
            jnp.stack(nk_p, axis=0), jnp.stack(nv_p, axis=0), jnp.stack(nc_p, axis=0), jnp.stack(np_p, axis=0),
            jnp.stack(nk_s, axis=0), jnp.stack(nv_s, axis=0), jnp.stack(nc_s, axis=0), jnp.stack(np_s, axis=0))
```

```python
import functools

import jax
import jax.numpy as jnp
import numpy as np
from jax import lax
from jax.experimental import pallas as pl
from jax.experimental.pallas import tpu as pltpu

F32 = jnp.float32
BF16 = jnp.bfloat16

D_MODEL = 1024
HEADS = 8
MOBA_DH = 64
MOBA_W = HEADS * MOBA_DH
MOBA_BLOCK = 256
MOBA_TOPK = 3
NOPE = 64
ROPE = 32
QK_DIM = NOPE + ROPE
V_DIM = 64
MLA_W = HEADS * V_DIM
Q_LORA = 384
KV_LORA = 256
PAGE = 128
ROPE_THETA = 10000.0
EPS = 1e-6
NEG = -1e30

LANES = 128
V7X_VMEM_BYTES = 64 * 1024 * 1024
VMEM_LIMIT = V7X_VMEM_BYTES - 8 * 1024 * 1024

TOKEN_TILE = 256
ATTN_TILE = MOBA_BLOCK
PAGES_PER_STEP = 16


def _dot(a, b):
    return jnp.dot(a, b, preferred_element_type=F32)


def _dot_nt(a, b):
    return lax.dot_general(a, b, (((1,), (1,)), ((), ())), preferred_element_type=F32)


def _dot_split(x, w):
    hi = x.astype(BF16)
    lo = (x - hi.astype(F32)).astype(BF16)
    return _dot(hi, w) + _dot(lo, w)


def _sigmoid(x):
    return 1.0 / (1.0 + jnp.exp(-x))


def _silu(x):
    return x * _sigmoid(x)


def _rope_rows(y, cos, sin_signed, half):
    lane = lax.broadcasted_iota(jnp.int32, (y.shape[0], LANES), 1)
    first = (lane % (2 * half)) < half
    outs = []
    for c in range(y.shape[1] // LANES):
        yc = y[:, c * LANES:(c + 1) * LANES]
        up = pltpu.roll(yc, LANES - half, axis=1)
        dn = pltpu.roll(yc, half, axis=1)
        outs.append(yc * cos + jnp.where(first, up, dn) * sin_signed)
    return jnp.concatenate(outs, axis=1)


def _front_kernel(with_kv, x_ref, ng_ref, w1_ref, w2_ref, w3_ref, wuqn_ref, wuqr_ref, wuk_ref, wuv_ref,
                  gq_ref, gk_ref, gcq_ref, gckv_ref, gqn_ref, gqr_ref, gkn_ref, gkr_ref,
                  cos64_ref, sin64_ref, cos32_ref, sin32_ref,
                  g64_ref, s64_ref, s32_ref, e64_ref, e32_ref, t32_ref,
                  qa_ref, ka_ref, va_ref, ga_ref, qn_ref, qr_ref, ckv_ref, kpe_ref, gb_ref, g1_ref, g2_ref,
                  *kv_refs):
    x = x_ref[...]
    h = (x * lax.rsqrt(jnp.mean(x * x, axis=-1, keepdims=True) + EPS) * ng_ref[...]).astype(BF16)

    z1 = _dot(h, w1_ref[...])
    g64 = g64_ref[...]
    cos64, sin64 = cos64_ref[...], sin64_ref[...]

    def head_norm64(z, gain):
        ss = _dot_split(z * z, g64)
        return z * lax.rsqrt(ss * (1.0 / MOBA_DH) + EPS) * gain

    qa = _rope_rows(head_norm64(z1[:, 0:MOBA_W], gq_ref[...]), cos64, sin64, MOBA_DH // 2)
    qa_ref[...] = (qa * MOBA_DH ** -0.5).astype(qa_ref.dtype)
    ka_ref[...] = _rope_rows(head_norm64(z1[:, MOBA_W:2 * MOBA_W], gk_ref[...]), cos64, sin64, MOBA_DH // 2)
    va_ref[...] = z1[:, 2 * MOBA_W:3 * MOBA_W]
    ga_ref[...] = z1[:, 3 * MOBA_W:4 * MOBA_W]

    z2 = _dot(h, w2_ref[...])
    cq = z2[:, 0:Q_LORA]
    cqn = (cq * lax.rsqrt(jnp.mean(cq * cq, axis=-1, keepdims=True) + EPS) * gcq_ref[...]).astype(BF16)
    qn = _dot(cqn, wuqn_ref[...])
    qr = _dot(cqn, wuqr_ref[...])
    s64, s32, e64, e32 = s64_ref[...], s32_ref[...], e64_ref[...], e32_ref[...]
    ss = _dot_split(qn * qn, s64) + _dot_split(qr * qr, s32)
    r = lax.rsqrt(ss * (1.0 / QK_DIM) + EPS)
    scale = QK_DIM ** -0.5
    cos32, sin32 = cos32_ref[...], sin32_ref[...]
    qn_ref[...] = (qn * _dot_split(r, e64) * (gqn_ref[...] * scale)).astype(qn_ref.dtype)
    qr_n = qr * _dot_split(r, e32) * (gqr_ref[...] * scale)
    qr_ref[...] = _rope_rows(qr_n, cos32, sin32, ROPE // 2).astype(qr_ref.dtype)

    ckv_raw = z2[:, Q_LORA:Q_LORA + KV_LORA]
    ckv = ckv_raw * lax.rsqrt(jnp.mean(ckv_raw * ckv_raw, axis=-1, keepdims=True) + EPS) * gckv_ref[...]
    ckv_ref[...] = ckv
    kpe_p = z2[:, Q_LORA + KV_LORA:Q_LORA + KV_LORA + LANES]
    kpe_ref[...] = kpe_p[:, 0:ROPE]

    z3 = _dot(h, w3_ref[...])
    gb_ref[...] = z3[:, 0:MLA_W]
    g1_ref[...] = z3[:, MLA_W:MLA_W + D_MODEL]
    g2_ref[...] = z3[:, MLA_W + D_MODEL:MLA_W + 2 * D_MODEL]

    if with_kv:
        kmn_ref, kmr_ref, vm_ref = kv_refs
        cb = ckv.astype(BF16)
        kn = _dot(cb, wuk_ref[...])
        ssk = _dot_split(kn * kn, s64) + jnp.sum(kpe_p * kpe_p, axis=-1, keepdims=True)
        rk = lax.rsqrt(ssk * (1.0 / QK_DIM) + EPS)
        kmn_ref[...] = (kn * _dot_split(rk, e64) * gkn_ref[...]).astype(BF16)
        kt = _dot_split(kpe_p, t32_ref[...]) * gkr_ref[...]
        kmr_ref[...] = (_rope_rows(kt, cos32, sin32, ROPE // 2) * _dot_split(rk, e32)).astype(BF16)
        vm_ref[...] = _dot(cb, wuv_ref[...]).astype(BF16)


def _const_spec(shape):
    nd = len(shape)
    return pl.BlockSpec(shape, lambda *_: (0,) * nd, pipeline_mode=pl.Buffered(1))


def _front(x, consts, tables, q_dtype, with_kv, table_rows):
    t = x.shape[0]
    tm = min(TOKEN_TILE, t)
    n_tab = table_rows // tm
    row = lambda w: pl.BlockSpec((tm, w), lambda i: (i, 0))
    tab = lambda: pl.BlockSpec((tm, LANES), lambda i: (i % n_tab, 0))
    in_specs = [row(D_MODEL)] + [_const_spec(c.shape) for c in consts[:16]] + [tab() for _ in range(4)] + \
               [_const_spec(c.shape) for c in consts[16:]]
    widths = [(MOBA_W, q_dtype), (MOBA_W, F32), (MOBA_W, F32), (MOBA_W, F32), (MLA_W, q_dtype),
              (HEADS * ROPE, q_dtype), (KV_LORA, F32), (ROPE, F32), (MLA_W, F32), (D_MODEL, F32), (D_MODEL, F32)]
    if with_kv:
        widths += [(MLA_W, BF16), (HEADS * ROPE, BF16), (MLA_W, BF16)]
    out_shape = [jax.ShapeDtypeStruct((t, w), d) for w, d in widths]
    out_specs = [row(w) for w, _ in widths]
    return pl.pallas_call(
        functools.partial(_front_kernel, with_kv),
        out_shape=out_shape,
        grid=(t // tm,),
        in_specs=in_specs,
        out_specs=out_specs,
        compiler_params=pltpu.CompilerParams(dimension_semantics=("arbitrary",), vmem_limit_bytes=VMEM_LIMIT),
        name="front_kv" if with_kv else "front",
    )(x, *consts[:16], *tables, *consts[16:])


def _flash_tiles(i, scores, values, causal):
    s = jnp.where(causal, scores(i), NEG)
    m = jnp.max(s, axis=-1, keepdims=True)
    p = jnp.exp(s - m)
    l = jnp.sum(p, axis=-1, keepdims=True)
    acc = _dot(p.astype(BF16), values(i))

    def body(n, carry):
        m, l, acc = carry
        s = scores(n)
        m2 = jnp.maximum(m, jnp.max(s, axis=-1, keepdims=True))
        a = jnp.exp(m - m2)
        p = jnp.exp(s - m2)
        return m2, a * l + jnp.sum(p, axis=-1, keepdims=True), a * acc + _dot(p.astype(BF16), values(n))

    m, l, acc = lax.fori_loop(0, i, body, (m, l, acc))
    return acc / l


def _moba_prompt_kernel(nb, topk, q_ref, k_ref, v_ref, g_ref, o_ref, kaug_ref, vb_ref, bmh_ref, bml_ref):
    i = pl.program_id(1)
    lane = lax.broadcasted_iota(jnp.int32, (ATTN_TILE, LANES), 1)

    @pl.when(i == 0)
    def _prepare_sequence():
        vb_ref[...] = v_ref[...].astype(BF16)
        means = [jnp.mean(k_ref[n * MOBA_BLOCK:(n + 1) * MOBA_BLOCK, :], axis=0, keepdims=True) if n < nb
                 else jnp.zeros((1, MOBA_W), F32) for n in range(8)]
        km = jnp.concatenate(means, axis=0)
        col = lax.broadcasted_iota(jnp.int32, (8, MOBA_W), 1)
        per_head = [jnp.where(col // MOBA_DH == h, km, 0.0) for h in range(HEADS)]
        bm = jnp.concatenate(per_head + per_head, axis=0)
        hi = bm.astype(BF16)
        bmh_ref[...] = hi
        bml_ref[...] = (bm - hi.astype(F32)).astype(BF16)
        for n in range(nb):
            for h in range(HEADS):
                p, half = divmod(h, 2)
                kp = k_ref[n * MOBA_BLOCK:(n + 1) * MOBA_BLOCK, p * LANES:(p + 1) * LANES]
                keep = (lane < MOBA_DH) if half == 0 else (lane >= MOBA_DH)
                hot = (MOBA_DH + 8 * h + n) if half == 0 else (8 * h + n)
                kaug_ref[h, n * MOBA_BLOCK:(n + 1) * MOBA_BLOCK, :] = jnp.where(
                    keep, kp, jnp.where(lane == hot, 1.0, 0.0)).astype(BF16)

    q = q_ref[...]
    gate = _dot_nt(q, bmh_ref[...]) + _dot_nt(q, bml_ref[...])
    nl = lane % 8
    valid = nl < i
    gm = jnp.where(valid, gate, NEG)
    rank = jnp.zeros(gm.shape, F32)
    for s in range(1, 8):
        below = pltpu.roll(gm, s, axis=1)
        above = pltpu.roll(gm, LANES - s, axis=1)
        rank = rank + jnp.where((nl >= s) & (below >= gm), 1.0, 0.0)
        rank = rank + jnp.where((nl + s < 8) & (above > gm), 1.0, 0.0)
    attend = (valid & (rank < topk)) | (nl == i)
    bias = jnp.where(attend, 0.0, NEG)

    row = lax.broadcasted_iota(jnp.int32, (ATTN_TILE, ATTN_TILE), 0)
    colk = lax.broadcasted_iota(jnp.int32, (ATTN_TILE, ATTN_TILE), 1)
    causal = colk <= row
    outs = []
    for p in range(HEADS // 2):
        qp = q[:, p * LANES:(p + 1) * LANES].astype(F32)
        pair = []
        for half in range(2):
            h = 2 * p + half
            if half == 0:
                q_aug = jnp.where(lane < MOBA_DH, qp, jnp.where(lane // 8 == 8 + h, bias, 0.0))
            else:
                q_aug = jnp.where(lane >= MOBA_DH, qp, jnp.where(lane // 8 == h, bias, 0.0))
            q_aug = q_aug.astype(BF16)

            def scores(n, q_aug=q_aug, h=h):
                st = pl.multiple_of(n * ATTN_TILE, ATTN_TILE)
                return _dot_nt(q_aug, kaug_ref[h, pl.ds(st, ATTN_TILE), :])

            def values(n, p=p):
                st = pl.multiple_of(n * ATTN_TILE, ATTN_TILE)
                return vb_ref[pl.ds(st, ATTN_TILE), p * LANES:(p + 1) * LANES]

            pair.append(_flash_tiles(i, scores, values, causal))
        outs.append(jnp.where(lane < MOBA_DH, pair[0], pair[1]))
    o = jnp.concatenate(outs, axis=1)
    o_ref[...] = (o * _silu(g_ref[...])).astype(o_ref.dtype)


def _moba_prompt(qa, ka, va, ga, batch, seq):
    nq = seq // ATTN_TILE
    nb = seq // MOBA_BLOCK
    tile = lambda w: pl.BlockSpec((ATTN_TILE, w), lambda b, i: (b * nq + i, 0))
    whole = lambda w: pl.BlockSpec((seq, w), lambda b, i: (b, 0))
    return pl.pallas_call(
        functools.partial(_moba_prompt_kernel, nb, min(MOBA_TOPK, nb)),
        out_shape=jax.ShapeDtypeStruct((batch * seq, MOBA_W), BF16),
        grid=(batch, nq),
        in_specs=[tile(MOBA_W), whole(MOBA_W), whole(MOBA_W), tile(MOBA_W)],
        out_specs=tile(MOBA_W),
        scratch_shapes=[pltpu.VMEM((HEADS, seq, LANES), BF16), pltpu.VMEM((seq, MOBA_W), BF16),
                        pltpu.VMEM((LANES, MOBA_W), BF16), pltpu.VMEM((LANES, MOBA_W), BF16)],
        compiler_params=pltpu.CompilerParams(dimension_semantics=("arbitrary", "arbitrary"),
                                             vmem_limit_bytes=VMEM_LIMIT),
        name="moba_prompt",
    )(qa, ka, va, ga)


def _mla_prompt_kernel(qn_ref, qr_ref, kn_ref, kr_ref, v_ref, g_ref, o_ref):
    i = pl.program_id(1)
    lane = lax.broadcasted_iota(jnp.int32, (ATTN_TILE, LANES), 1)
    lane1 = lax.broadcasted_iota(jnp.int32, (1, LANES), 1)
    row = lax.broadcasted_iota(jnp.int32, (ATTN_TILE, ATTN_TILE), 0)
    colk = lax.broadcasted_iota(jnp.int32, (ATTN_TILE, ATTN_TILE), 1)
    causal = colk <= row
    qn = qn_ref[...]
    qr = qr_ref[...]
    outs = []
    for p in range(HEADS // 2):
        pair = []
        for half in range(2):
            h = 2 * p + half
            quad, sub = divmod(h, LANES // ROPE)
            mask_n = jnp.where(lane1 // NOPE == half, 1.0, 0.0).astype(BF16)
            mask_r = jnp.where(lane1 // ROPE == sub, 1.0, 0.0).astype(BF16)
            qnm = qn[:, p * LANES:(p + 1) * LANES] * mask_n
            qrm = qr[:, quad * LANES:(quad + 1) * LANES] * mask_r

            def scores(n, qnm=qnm, qrm=qrm, p=p, quad=quad):
                st = pl.multiple_of(n * ATTN_TILE, ATTN_TILE)
                return (_dot_nt(qnm, kn_ref[pl.ds(st, ATTN_TILE), p * LANES:(p + 1) * LANES])
                        + _dot_nt(qrm, kr_ref[pl.ds(st, ATTN_TILE), quad * LANES:(quad + 1) * LANES]))

            def values(n, p=p):
                st = pl.multiple_of(n * ATTN_TILE, ATTN_TILE)
                return v_ref[pl.ds(st, ATTN_TILE), p * LANES:(p + 1) * LANES]

            pair.append(_flash_tiles(i, scores, values, causal))
        outs.append(jnp.where(lane < V_DIM, pair[0], pair[1]))
    o = jnp.concatenate(outs, axis=1)
    o_ref[...] = (o * _silu(g_ref[...])).astype(o_ref.dtype)


def _mla_prompt(qn, qr, kmn, kmr, vm, gb, batch, seq):
    nq = seq // ATTN_TILE
    tile = lambda w: pl.BlockSpec((ATTN_TILE, w), lambda b, i: (b * nq + i, 0))
    whole = lambda w: pl.BlockSpec((seq, w), lambda b, i: (b, 0))
    return pl.pallas_call(
        _mla_prompt_kernel,
        out_shape=jax.ShapeDtypeStruct((batch * seq, MLA_W), BF16),
        grid=(batch, nq),
        in_specs=[tile(MLA_W), tile(HEADS * ROPE), whole(MLA_W), whole(HEADS * ROPE), whole(MLA_W), tile(MLA_W)],
        out_specs=tile(MLA_W),
        compiler_params=pltpu.CompilerParams(dimension_semantics=("arbitrary", "arbitrary"),
                                             vmem_limit_bytes=VMEM_LIMIT),
        name="mla_prompt",
    )(qn, qr, kmn, kmr, vm, gb)


def _merge_kernel(n_gates, x_ref, oa_ref, ob_ref, *refs):
    gate_refs = refs[:n_gates]
    g1_ref, g2_ref, wa_ref, wb_ref, wo_ref, y_ref = refs[n_gates:]
    oa, ob = oa_ref[...], ob_ref[...]
    if n_gates:
        oa = (oa * _silu(gate_refs[0][...])).astype(BF16)
        ob = (ob * _silu(gate_refs[1][...])).astype(BF16)
    a = _dot(oa, wa_ref[...])
    b = _dot(ob, wb_ref[...])
    hm = _sigmoid(g1_ref[...]) * a + _sigmoid(g2_ref[...]) * b
    y_ref[...] = x_ref[...] + _dot(hm.astype(BF16), wo_ref[...])


def _merge(x, oa, ob, branch_gates, g1, g2, wa, wb, wo):
    t = x.shape[0]
    tm = min(TOKEN_TILE, t)
    row = lambda w: pl.BlockSpec((tm, w), lambda i: (i, 0))
    return pl.pallas_call(
        functools.partial(_merge_kernel, len(branch_gates)),
        out_shape=jax.ShapeDtypeStruct((t, D_MODEL), F32),
        grid=(t // tm,),
        in_specs=[row(D_MODEL), row(MOBA_W), row(MLA_W)] + [row(g.shape[1]) for g in branch_gates] +
                 [row(D_MODEL), row(D_MODEL), _const_spec(wa.shape), _const_spec(wb.shape), _const_spec(wo.shape)],
        out_specs=row(D_MODEL),
        compiler_params=pltpu.CompilerParams(dimension_semantics=("arbitrary",), vmem_limit_bytes=VMEM_LIMIT),
        name="merge_sample" if branch_gates else "merge_prompt",
    )(x, oa, ob, *branch_gates, g1, g2, wa, wb, wo)


def _moba_select_kernel(ppc, pt_ref, q_ref, *refs):
    page_refs = refs[:ppc]
    sel_ref, qb_ref, gate_ref = refs[ppc:]
    c = pl.program_id(1)
    lane = lax.broadcasted_iota(jnp.int32, (HEADS, LANES), 1)
    low = jnp.float32(-3.0e38)

    @pl.when(c == 0)
    def _start_sequence():
        qb_ref[...] = jnp.broadcast_to(q_ref[...], (HEADS, MOBA_DH, LANES))
        gate_ref[...] = jnp.full((HEADS, LANES), low, F32)

    qb = qb_ref[...]
    gate = gate_ref[...]
    blocks_per_step = ppc // 2
    for j in range(blocks_per_step):
        ksum = page_refs[2 * j][...] + page_refs[2 * j + 1][...]
        part = jnp.sum(ksum * qb, axis=1)
        g = jnp.sum(part, axis=1, keepdims=True) * (1.0 / MOBA_BLOCK)
        gate = jnp.where(lane == c * blocks_per_step + j, g, gate)
    gate_ref[...] = gate

    @pl.when(c == pl.num_programs(1) - 1)
    def _pick_blocks():
        gw = gate
        out = jnp.zeros((HEADS, LANES), jnp.int32)
        for r in range(MOBA_TOPK):
            mx = jnp.max(gw, axis=1, keepdims=True)
            idx = jnp.min(jnp.where(gw == mx, lane, LANES), axis=1, keepdims=True)
            out = jnp.where(lane == r, idx, out)
            gw = jnp.where(lane == idx, low, gw)
        sel_ref[...] = out


def _moba_select(page_table, q_col, kt_pages):
    batch, n_pages = page_table.shape
    ppc = min(PAGES_PER_STEP, n_pages)
    page_spec = lambda k: pl.BlockSpec((None, HEADS, MOBA_DH, PAGE), lambda b, c, pt: (pt[b, c * ppc + k], 0, 0, 0))
    grid_spec = pltpu.PrefetchScalarGridSpec(
        num_scalar_prefetch=1,
        grid=(batch, n_pages // ppc),
        in_specs=[pl.BlockSpec((None, HEADS, MOBA_DH, 1), lambda b, c, pt: (b, 0, 0, 0))] +
                 [page_spec(k) for k in range(ppc)],
        out_specs=pl.BlockSpec((None, HEADS, LANES), lambda b, c, pt: (b, 0, 0)),
        scratch_shapes=[pltpu.VMEM((HEADS, MOBA_DH, LANES), F32), pltpu.VMEM((HEADS, LANES), F32)],
    )
    return pl.pallas_call(
        functools.partial(_moba_select_kernel, ppc),
        out_shape=jax.ShapeDtypeStruct((batch, HEADS, LANES), jnp.int32),
        grid_spec=grid_spec,
        compiler_params=pltpu.CompilerParams(dimension_semantics=("arbitrary", "arbitrary"),
                                             vmem_limit_bytes=VMEM_LIMIT),
        name="moba_select",
    )(page_table, q_col, *([kt_pages] * ppc))


SEL_PAGES = 2 * MOBA_TOPK


def _moba_attend_kernel(pt_ref, sel_ref, q_ref, kn_ref, vn_ref, kt_hbm, vt_hbm, o_ref, kbuf, vbuf, sem):
    b = pl.program_id(0)
    slot = b % 2

    def copies(seq, slot):
        out = []
        for h in range(HEADS):
            for r in range(MOBA_TOPK):
                blk = sel_ref[seq, h * MOBA_TOPK + r]
                for pg in range(2):
                    page = pt_ref[seq, 2 * blk + pg]
                    out.append(pltpu.make_async_copy(kt_hbm.at[page, h], kbuf.at[slot, h, 2 * r + pg], sem.at[0, slot]))
                    out.append(pltpu.make_async_copy(vt_hbm.at[page, h], vbuf.at[slot, h, 2 * r + pg], sem.at[1, slot]))
        return out

    @pl.when(b == 0)
    def _prime():
        for cp in copies(0, 0):
            cp.start()

    @pl.when(b + 1 < pl.num_programs(0))
    def _prefetch_next():
        for cp in copies(b + 1, 1 - slot):
            cp.start()

    for cp in copies(b, slot):
        cp.wait()

    q = q_ref[...]
    k_new = kn_ref[...]
    v_new = vn_ref[...]
    s_new = jnp.sum(q * k_new, axis=1)
    for h in range(HEADS):
        qb = jnp.broadcast_to(q[h], (MOBA_DH, LANES))
        s = jnp.concatenate([jnp.sum(kbuf[slot, h, j] * qb, axis=0, keepdims=True) for j in range(SEL_PAGES)], axis=0)
        sn = s_new[h:h + 1, :]
        m = jnp.maximum(jnp.max(jnp.max(s, axis=1, keepdims=True), axis=0, keepdims=True), sn)
        p = jnp.exp(s - m)
        pn = jnp.exp(sn - m)
        l = jnp.sum(jnp.sum(p, axis=1, keepdims=True), axis=0, keepdims=True) + pn
        acc = jnp.zeros((MOBA_DH, LANES), F32)
        for j in range(SEL_PAGES):
            acc = acc + vbuf[slot, h, j] * p[j:j + 1, :]
        o_ref[h] = (jnp.sum(acc, axis=1, keepdims=True) + pn * v_new[h]) / l


def _moba_attend(page_table, sel, q_col, k_col, v_col, kt_pages, vt_pages):
    batch = page_table.shape[0]
    col = lambda: pl.BlockSpec((None, HEADS, MOBA_DH, 1), lambda b, pt, sl: (b, 0, 0, 0))
    grid_spec = pltpu.PrefetchScalarGridSpec(
        num_scalar_prefetch=2,
        grid=(batch,),
        in_specs=[col(), col(), col(), pl.BlockSpec(memory_space=pl.ANY), pl.BlockSpec(memory_space=pl.ANY)],
        out_specs=col(),
        scratch_shapes=[pltpu.VMEM((2, HEADS, SEL_PAGES, MOBA_DH, PAGE), F32),
                        pltpu.VMEM((2, HEADS, SEL_PAGES, MOBA_DH, PAGE), F32),
                        pltpu.SemaphoreType.DMA((2, 2))],
    )
    return pl.pallas_call(
        _moba_attend_kernel,
        out_shape=jax.ShapeDtypeStruct((batch, HEADS, MOBA_DH, 1), F32),
        grid_spec=grid_spec,
        compiler_params=pltpu.CompilerParams(dimension_semantics=("arbitrary",), vmem_limit_bytes=VMEM_LIMIT),
        name="moba_attend",
    )(page_table, sel, q_col, k_col, v_col, kt_pages, vt_pages)


MLA_SUB_PAGES = 4


def _mla_decode_kernel(ppc, pt_ref, qn_ref, qr_ref, cnew_ref, pnew_ref, wukt_ref, wuv_ref, gkn_ref, gkr_ref,
                       cos_ref, sin_ref, cosn_ref, sinn_ref, *refs):
    ckv_refs = refs[:ppc]
    kpe_refs = refs[ppc:2 * ppc]
    o_ref, a_ref, m_ref, l_ref, acc_ref = refs[2 * ppc:]
    b = pl.program_id(0)
    c = pl.program_id(1)
    lhs_rows = HEADS * NOPE

    @pl.when((b == 0) & (c == 0))
    def _load_weights():
        a_ref[0:lhs_rows, :] = wukt_ref[...]

    @pl.when(c == 0)
    def _start_sequence():
        rowh = lax.broadcasted_iota(jnp.int32, (HEADS, MLA_W), 0)
        colh = lax.broadcasted_iota(jnp.int32, (HEADS, MLA_W), 1)
        qg = jnp.where(colh // NOPE == rowh, qn_ref[...] * gkn_ref[...], 0.0).astype(BF16)
        qlat = _dot(qg, wukt_ref[...])
        a_ref[lhs_rows:lhs_rows + 16, :] = jnp.concatenate([qlat, jnp.zeros_like(qlat)], axis=0).astype(BF16)
        m_ref[...] = jnp.full(m_ref.shape, NEG, F32)
        l_ref[...] = jnp.zeros(l_ref.shape, F32)
        acc_ref[...] = jnp.zeros(acc_ref.shape, F32)

    qr = qr_ref[...].astype(BF16)
    gkr = gkr_ref[...]

    def attend(cb, kpt, cos, sin, key_ok):
        kt = _dot_nt(a_ref[...], cb)
        tk = cb.shape[0]
        kn = kt[0:lhs_rows, :]
        ssn = jnp.sum((kn * kn).reshape(HEADS, NOPE, tk), axis=1)
        sn = kt[lhs_rows:lhs_rows + HEADS, :]
        n2 = jnp.sum(kpt * kpt, axis=0, keepdims=True)
        y = kpt * gkr
        y1, y2 = y[0:ROPE // 2, :], y[ROPE // 2:ROPE, :]
        kr = jnp.concatenate([y1 * cos - y2 * sin, y2 * cos + y1 * sin], axis=0)
        sr = _dot(qr, kr.astype(BF16))
        s = (sn + sr) * lax.rsqrt((ssn + n2) * (1.0 / QK_DIM) + EPS)
        if key_ok is not None:
            s = jnp.where(key_ok, s, NEG)
        m = m_ref[...]
        m2 = jnp.maximum(m, jnp.max(s, axis=1, keepdims=True))
        a = jnp.exp(m - m2)
        p = jnp.exp(s - m2)
        l_ref[...] = a * l_ref[...] + jnp.sum(p, axis=1, keepdims=True)
        acc_ref[...] = a * acc_ref[...] + _dot(p.astype(BF16), cb)
        m_ref[...] = m2

    sub = min(MLA_SUB_PAGES, ppc)
    for t in range(ppc // sub):
        pages = range(t * sub, (t + 1) * sub)
        cb = jnp.concatenate([ckv_refs[k][...].astype(BF16) for k in pages], axis=0)
        kpt = jnp.concatenate([kpe_refs[k][...] for k in pages], axis=1)
        lo, hi = t * sub * PAGE, (t + 1) * sub * PAGE
        attend(cb, kpt, cos_ref[:, lo:hi], sin_ref[:, lo:hi], None)

    @pl.when(c == pl.num_programs(1) - 1)
    def _finish_sequence():
        rowi = lax.broadcasted_iota(jnp.int32, (PAGE, KV_LORA), 0)
        lanei = lax.broadcasted_iota(jnp.int32, (ROPE, PAGE), 1)
        cb = jnp.where(rowi == 0, jnp.broadcast_to(cnew_ref[...], (PAGE, KV_LORA)), 0.0).astype(BF16)
        kpt = jnp.where(lanei == 0, jnp.broadcast_to(pnew_ref[...], (ROPE, PAGE)), 0.0)
        key_ok = lax.broadcasted_iota(jnp.int32, (HEADS, PAGE), 1) == 0
        attend(cb, kpt, cosn_ref[...], sinn_ref[...], key_ok)
        ol = (acc_ref[...] / l_ref[...]).astype(BF16)
        res = _dot(ol, wuv_ref[...])
        rowh = lax.broadcasted_iota(jnp.int32, (HEADS, MLA_W), 0)
        colh = lax.broadcasted_iota(jnp.int32, (HEADS, MLA_W), 1)
        o_ref[...] = jnp.sum(jnp.where(colh // V_DIM == rowh, res, 0.0), axis=0, keepdims=True)


def _mla_decode(page_table, qn, qr, c_new, p_new, wukt, wuv, gkn, gkr_col, cos_t, sin_t, cos_n, sin_n,
                ckv_pages, kpet_pages):
    batch, n_pages = page_table.shape
    ppc = min(PAGES_PER_STEP, n_pages)
    tk = ppc * PAGE
    seq_spec = lambda shape: pl.BlockSpec((None,) + shape, lambda b, c, pt: (b,) + (0,) * len(shape))
    const = lambda arr: pl.BlockSpec(arr.shape, lambda b, c, pt: (0,) * arr.ndim)
    ckv_spec = lambda k: pl.BlockSpec((None, PAGE, KV_LORA), lambda b, c, pt: (pt[b, c * ppc + k], 0, 0))
    kpe_spec = lambda k: pl.BlockSpec((None, ROPE, PAGE), lambda b, c, pt: (pt[b, c * ppc + k], 0, 0))
    tab_spec = pl.BlockSpec((ROPE // 2, tk), lambda b, c, pt: (0, c))
    grid_spec = pltpu.PrefetchScalarGridSpec(
        num_scalar_prefetch=1,
        grid=(batch, n_pages // ppc),
        in_specs=[seq_spec((1, MLA_W)), seq_spec((HEADS, ROPE)), seq_spec((1, KV_LORA)), seq_spec((ROPE, 1)),
                  const(wukt), const(wuv), const(gkn), const(gkr_col), tab_spec, tab_spec, const(cos_n), const(sin_n)] +
                 [ckv_spec(k) for k in range(ppc)] + [kpe_spec(k) for k in range(ppc)],
        out_specs=seq_spec((1, MLA_W)),
        scratch_shapes=[pltpu.VMEM((HEADS * NOPE + 16, KV_LORA), BF16), pltpu.VMEM((HEADS, 1), F32),
                        pltpu.VMEM((HEADS, 1), F32), pltpu.VMEM((HEADS, KV_LORA), F32)],
    )
    return pl.pallas_call(
        functools.partial(_mla_decode_kernel, ppc),
        out_shape=jax.ShapeDtypeStruct((batch, 1, MLA_W), F32),
        grid_spec=grid_spec,
        compiler_params=pltpu.CompilerParams(dimension_semantics=("arbitrary", "arbitrary"),
                                             vmem_limit_bytes=VMEM_LIMIT),
        name="mla_decode",
    )(page_table, qn, qr, c_new, p_new, wukt, wuv, gkn, gkr_col, cos_t, sin_t, cos_n, sin_n,
      *([ckv_pages] * ppc), *([kpet_pages] * ppc))


def _rope_angles(pos, half):
    inv = np.power(np.float32(ROPE_THETA), -np.arange(half, dtype=np.float32) / np.float32(half))
    return np.asarray(pos, np.float32)[:, None] * inv[None, :]


def _row_tables(pos, half):
    ang = _rope_angles(pos, half)
    k = np.arange(LANES) % (2 * half)
    cos = np.cos(ang)[:, k % half]
    sin = np.sin(ang)[:, k % half] * np.where(k < half, -1.0, 1.0).astype(np.float32)[None, :]
    return jnp.asarray(cos, F32), jnp.asarray(sin, F32)


def _block_ones(rows_per_group, groups, cols):
    m = np.zeros((groups * rows_per_group, cols), np.float32)
    m[np.arange(groups * rows_per_group), np.arange(groups * rows_per_group) // rows_per_group] = 1.0
    return m


def kernel(x_prompt, x_sample, cache_moba_k, cache_moba_v, cache_mla_ckv, cache_mla_kpe, page_table, norm_gain, w_in, moba_q_gain, moba_k_gain, mla_q_lat_gain, w_uq, mla_q_gain, mla_kv_lat_gain, w_uk, w_uv, mla_k_gain, w_branch_a, w_branch_b, w_out):
    batch, seq, _ = x_prompt.shape
    dec_batch, dec_seq, _ = x_sample.shape
    depth = norm_gain.shape[0]
    assert depth == 1 and dec_seq == 1
    n_pages = page_table.shape[1]
    past = n_pages * PAGE
    assert seq % ATTN_TILE == 0 and seq // MOBA_BLOCK <= 8 and past % MOBA_BLOCK == 0
    assert past // MOBA_BLOCK >= MOBA_TOPK and n_pages % min(PAGES_PER_STEP, n_pages) == 0

    w = w_in[0]
    o = 4 * MOBA_W
    w1 = w[:, 0:o].astype(BF16)
    w2 = jnp.pad(w[:, o:o + Q_LORA + KV_LORA + ROPE], ((0, 0), (0, LANES - ROPE))).astype(BF16)
    o += Q_LORA + KV_LORA + ROPE
    w3 = w[:, o:o + MLA_W + 2 * D_MODEL].astype(BF16)
    wuqn = w_uq[0][:, :, :NOPE].reshape(Q_LORA, HEADS * NOPE).astype(BF16)
    wuqr = w_uq[0][:, :, NOPE:].reshape(Q_LORA, HEADS * ROPE).astype(BF16)
    wuk = w_uk[0].reshape(KV_LORA, HEADS * NOPE).astype(BF16)
    wuv = w_uv[0].reshape(KV_LORA, MLA_W).astype(BF16)
    tile8 = lambda g: jnp.tile(g, HEADS)[None, :].astype(F32)
    consts = [
        norm_gain[0][None, :], w1, w2, w3, wuqn, wuqr, wuk, wuv,
        tile8(moba_q_gain[0]), tile8(moba_k_gain[0]), mla_q_lat_gain[0][None, :], mla_kv_lat_gain[0][None, :],
        tile8(mla_q_gain[0][:NOPE]), tile8(mla_q_gain[0][NOPE:]), tile8(mla_k_gain[0][:NOPE]), tile8(mla_k_gain[0][NOPE:]),
        jnp.asarray(_block_ones(MOBA_DH, HEADS, HEADS) @ _block_ones(MOBA_DH, HEADS, HEADS).T, BF16),
        jnp.asarray(_block_ones(NOPE, HEADS, LANES), BF16),
        jnp.asarray(_block_ones(ROPE, HEADS, LANES), BF16),
        jnp.asarray(_block_ones(NOPE, HEADS, LANES).T, BF16),
        jnp.asarray(_block_ones(ROPE, HEADS, LANES).T, BF16),
        jnp.asarray(np.concatenate([np.tile(np.eye(ROPE, dtype=np.float32), (1, HEADS)),
                                    np.zeros((LANES - ROPE, HEADS * ROPE), np.float32)], axis=0), BF16),
    ]
    pos_p = np.arange(seq)
    pos_s = np.full((dec_batch,), past)
    tables_p = _row_tables(pos_p, MOBA_DH // 2) + _row_tables(pos_p, ROPE // 2)
    tables_s = _row_tables(pos_s, MOBA_DH // 2) + _row_tables(pos_s, ROPE // 2)
    wa = w_branch_a[0].astype(BF16)
    wb = w_branch_b[0].astype(BF16)
    wo = w_out[0].astype(BF16)

    xp = x_prompt.reshape(batch * seq, D_MODEL)
    (qa, ka, va, ga, qn, qr, ckv, kpe, gb, g1, g2, kmn, kmr, vm) = _front(xp, consts, tables_p, BF16, True, seq)
    oa = _moba_prompt(qa, ka, va, ga, batch, seq)
    ob = _mla_prompt(qn, qr, kmn, kmr, vm, gb, batch, seq)
    y_prompt = _merge(xp, oa, ob, (), g1, g2, wa, wb, wo).reshape(batch, seq, D_MODEL)

    xs = x_sample.reshape(dec_batch, D_MODEL)
    (qa_s, ka_s, va_s, ga_s, qn_s, qr_s, ckv_s, kpe_s, gb_s, g1_s, g2_s) = _front(xs, consts, tables_s, F32, False, dec_batch)
    kt_pages = jnp.transpose(cache_moba_k[0], (0, 2, 3, 1))
    vt_pages = jnp.transpose(cache_moba_v[0], (0, 2, 3, 1))
    ckv_pages = cache_mla_ckv[0]
    kpet_pages = jnp.transpose(cache_mla_kpe[0], (0, 2, 1))
    col = lambda a: a.reshape(dec_batch, HEADS, MOBA_DH, 1)
    q_col = col(qa_s)
    sel = _moba_select(page_table, q_col, kt_pages)[:, :, :MOBA_TOPK].reshape(dec_batch, HEADS * MOBA_TOPK)
    oa_s = _moba_attend(page_table, sel, q_col, col(ka_s), col(va_s), kt_pages, vt_pages).reshape(dec_batch, MOBA_W)

    ang_past = _rope_angles(np.arange(past), ROPE // 2).T
    ang_new = np.broadcast_to(_rope_angles(np.array([past]), ROPE // 2).T, (ROPE // 2, PAGE))
    wukt = jnp.transpose(w_uk[0].reshape(KV_LORA, HEADS * NOPE)).astype(BF16)
    ob_s = _mla_decode(
        page_table, qn_s.reshape(dec_batch, 1, MLA_W), qr_s.reshape(dec_batch, HEADS, ROPE),
        ckv_s.reshape(dec_batch, 1, KV_LORA), kpe_s.reshape(dec_batch, ROPE, 1),
        wukt, wuv, tile8(mla_k_gain[0][:NOPE]), mla_k_gain[0][NOPE:].reshape(ROPE, 1),
        jnp.asarray(np.cos(ang_past), F32), jnp.asarray(np.sin(ang_past), F32),
        jnp.asarray(np.cos(ang_new), F32), jnp.asarray(np.sin(ang_new), F32),
        ckv_pages, kpet_pages).reshape(dec_batch, MLA_W)
    y_sample = _merge(xs, oa_s, ob_s, (ga_s, gb_s), g1_s, g2_s, wa, wb, wo).reshape(dec_batch, 1, D_MODEL)

    p5 = lambda a, w: a.reshape(1, batch, seq, HEADS, w)
    s5 = lambda a, w: a.reshape(1, dec_batch, 1, HEADS, w)
    return (y_prompt, y_sample,
            p5(ka, MOBA_DH), p5(va, MOBA_DH), ckv.reshape(1, batch, seq, KV_LORA), kpe.reshape(1, batch, seq, ROPE),
            s5(ka_s, MOBA_DH), s5(va_s, MOBA_DH), ckv_s.reshape(1, dec_batch, 1, KV_LORA),
            kpe_s.reshape(1, dec_batch, 1, ROPE))
```

```python
import functools

import jax
import jax.numpy as jnp
import numpy as np
from jax import lax
from jax.experimental import pallas as pl
from jax.experimental.pallas import tpu as pltpu

F32 = jnp.float32
BF16 = jnp.bfloat16

D_MODEL = 1024
HEADS = 8
MOBA_DH = 64
MOBA_W = HEADS * MOBA_DH
MOBA_BLOCK = 256
MOBA_TOPK = 3
NOPE = 64
ROPE = 32
QK_DIM = NOPE + ROPE
V_DIM = 64
MLA_W = HEADS * V_DIM
Q_LORA = 384
KV_LORA = 256
PAGE = 128
ROPE_THETA = 10000.0
EPS = 1e-6
NEG = -1e30

LANES = 128
V7X_VMEM_BYTES = 64 * 1024 * 1024
VMEM_LIMIT = V7X_VMEM_BYTES - 8 * 1024 * 1024

TOKEN_TILE = 256
ATTN_TILE = MOBA_BLOCK
PAGES_PER_STEP = 16


def _dot(a, b):
    return jnp.dot(a, b, preferred_element_type=F32)


def _dot_nt(a, b):
    return lax.dot_general(a, b, (((1,), (1,)), ((), ())), preferred_element_type=F32)


def _dot_split(x, w):
    hi = x.astype(BF16)
    lo = (x - hi.astype(F32)).astype(BF16)
    return _dot(hi, w) + _dot(lo, w)


def _sigmoid(x):
    return 1.0 / (1.0 + jnp.exp(-x))


def _silu(x):
    return x * _sigmoid(x)


def _rope_rows(y, cos, sin_signed, half):
    lane = lax.broadcasted_iota(jnp.int32, (y.shape[0], LANES), 1)
    first = (lane % (2 * half)) < half
    outs = []
    for c in range(y.shape[1] // LANES):
        yc = y[:, c * LANES:(c + 1) * LANES]
        up = pltpu.roll(yc, LANES - half, axis=1)
        dn = pltpu.roll(yc, half, axis=1)
        outs.append(yc * cos + jnp.where(first, up, dn) * sin_signed)
    return jnp.concatenate(outs, axis=1)


def _front_kernel(prompt, x_ref, ng_ref, wqk_ref, wv_ref, wg_ref, w2_ref, w3_ref, wuqn_ref, wuqr_ref, wuk_ref, wuv_ref,
                  gq_ref, gk_ref, gcq_ref, gckv_ref, gqn_ref, gqr_ref, gkn_ref, gkr_ref,
                  cos64_ref, sin64_ref, cos32_ref, sin32_ref,
                  g64_ref, s64_ref, s32_ref, e64_ref, e32_ref, t32_ref, pn_ref, pr_ref,
                  qa_ref, ka_ref, v_ref, ga_ref, ckv_ref, kpe_ref, gb_ref, g1_ref, g2_ref, *mode_refs):
    x = x_ref[...]
    h = (x * lax.rsqrt(jnp.mean(x * x, axis=-1, keepdims=True) + EPS) * ng_ref[...]).astype(BF16)

    z1 = _dot(h, wqk_ref[...])
    g64 = g64_ref[...]
    cos64, sin64 = cos64_ref[...], sin64_ref[...]

    def head_norm64(z, gain):
        ss = _dot_split(z * z, g64)
        return z * lax.rsqrt(ss * (1.0 / MOBA_DH) + EPS) * gain

    qa = _rope_rows(head_norm64(z1[:, 0:MOBA_W], gq_ref[...]), cos64, sin64, MOBA_DH // 2)
    qa_ref[...] = (qa * MOBA_DH ** -0.5).astype(qa_ref.dtype)
    ka_ref[...] = _rope_rows(head_norm64(z1[:, MOBA_W:2 * MOBA_W], gk_ref[...]), cos64, sin64, MOBA_DH // 2)
    if prompt:
        v_ref[...] = _dot_nt(wv_ref[...], h)
    else:
        v_ref[...] = _dot(h, wv_ref[...])
    ga_ref[...] = _dot(h, wg_ref[...])

    z2 = _dot(h, w2_ref[...])
    cq = z2[:, 0:Q_LORA]
    cqn = (cq * lax.rsqrt(jnp.mean(cq * cq, axis=-1, keepdims=True) + EPS) * gcq_ref[...]).astype(BF16)
    qn = _dot(cqn, wuqn_ref[...])
    qr = _dot(cqn, wuqr_ref[...])
    s64, s32, e64, e32 = s64_ref[...], s32_ref[...], e64_ref[...], e32_ref[...]
    ss = _dot_split(qn * qn, s64) + _dot_split(qr * qr, s32)
    r = lax.rsqrt(ss * (1.0 / QK_DIM) + EPS)
    scale = QK_DIM ** -0.5
    cos32, sin32 = cos32_ref[...], sin32_ref[...]
    qn_n = qn * _dot_split(r, e64) * (gqn_ref[...] * scale)
    qr_n = _rope_rows(qr * _dot_split(r, e32) * (gqr_ref[...] * scale), cos32, sin32, ROPE // 2)

    ckv_raw = z2[:, Q_LORA:Q_LORA + KV_LORA]
    ckv = ckv_raw * lax.rsqrt(jnp.mean(ckv_raw * ckv_raw, axis=-1, keepdims=True) + EPS) * gckv_ref[...]
    ckv_ref[...] = ckv
    kpe_p = z2[:, Q_LORA + KV_LORA:Q_LORA + KV_LORA + LANES]
    kpe_ref[...] = kpe_p[:, 0:ROPE]

    z3 = _dot(h, w3_ref[...])
    gb_ref[...] = z3[:, 0:MLA_W]
    g1_ref[...] = z3[:, MLA_W:MLA_W + D_MODEL]
    g2_ref[...] = z3[:, MLA_W + D_MODEL:MLA_W + 2 * D_MODEL]

    if not prompt:
        qn_ref, qr_ref = mode_refs
        qn_ref[...] = qn_n
        qr_ref[...] = qr_n
        return
    qcat_ref, kcat_ref, vmt_ref = mode_refs
    pn, pr = pn_ref[...], pr_ref[...]
    qcat_ref[...] = (_dot(qn_n.astype(BF16), pn) + _dot(qr_n.astype(BF16), pr)).astype(BF16)
    cb = ckv.astype(BF16)
    kn = _dot(cb, wuk_ref[...])
    ssk = _dot_split(kn * kn, s64) + jnp.sum(kpe_p * kpe_p, axis=-1, keepdims=True)
    rk = lax.rsqrt(ssk * (1.0 / QK_DIM) + EPS)
    kn_n = (kn * _dot_split(rk, e64) * gkn_ref[...]).astype(BF16)
    kt = _dot_split(kpe_p, t32_ref[...]) * gkr_ref[...]
    kr_n = (_rope_rows(kt, cos32, sin32, ROPE // 2) * _dot_split(rk, e32)).astype(BF16)
    kcat_ref[...] = (_dot(kn_n, pn) + _dot(kr_n, pr)).astype(BF16)
    vmt_ref[...] = _dot_nt(wuv_ref[...], cb).astype(BF16)


def _const_spec(shape):
    nd = len(shape)
    return pl.BlockSpec(shape, lambda *_: (0,) * nd, pipeline_mode=pl.Buffered(1))


N_FRONT_WEIGHTS = 18


def _front(x, consts, tables, prompt, seq):
    t = x.shape[0]
    tm = min(TOKEN_TILE, t)
    n_tab = tables[0].shape[0] // tm
    row = lambda w: pl.BlockSpec((tm, w), lambda i: (i, 0))
    tab = lambda: pl.BlockSpec((tm, LANES), lambda i: (i % n_tab, 0))
    in_specs = [row(D_MODEL)] + [_const_spec(c.shape) for c in consts[:N_FRONT_WEIGHTS]] + [tab() for _ in range(4)] + \
               [_const_spec(c.shape) for c in consts[N_FRONT_WEIGHTS:]]
    q_dtype = BF16 if prompt else F32
    rows = lambda w, d: (jax.ShapeDtypeStruct((t, w), d), row(w))
    if prompt:
        nq = seq // tm
        cols = lambda d: (jax.ShapeDtypeStruct((t // seq, MOBA_W, seq), d),
                          pl.BlockSpec((None, MOBA_W, tm), lambda i: (i // nq, 0, i % nq)))
        v_out = cols(F32)
        mode_outs = [rows(HEADS * LANES, BF16), rows(HEADS * LANES, BF16), cols(BF16)]
    else:
        v_out = rows(MOBA_W, F32)
        mode_outs = [rows(MLA_W, F32), rows(HEADS * ROPE, F32)]
    outs = [rows(MOBA_W, q_dtype), rows(MOBA_W, F32), v_out, rows(MOBA_W, F32), rows(KV_LORA, F32), rows(ROPE, F32),
            rows(MLA_W, F32), rows(D_MODEL, F32), rows(D_MODEL, F32)] + mode_outs
    return pl.pallas_call(
        functools.partial(_front_kernel, prompt),
        out_shape=[o[0] for o in outs],
        grid=(t // tm,),
        in_specs=in_specs,
        out_specs=[o[1] for o in outs],
        compiler_params=pltpu.CompilerParams(dimension_semantics=("arbitrary",), vmem_limit_bytes=VMEM_LIMIT),
        name="front_prompt" if prompt else "front_sample",
    )(x, *consts[:N_FRONT_WEIGHTS], *tables, *consts[N_FRONT_WEIGHTS:])


def _flash_heads(i, scores_t, values_t, m_ref, l_ref, acc_ref, s_ref, p_ref):
    m_ref[...] = jnp.full(m_ref.shape, NEG, F32)
    l_ref[...] = jnp.zeros(l_ref.shape, F32)
    acc_ref[...] = jnp.zeros(acc_ref.shape, F32)
    key = lax.broadcasted_iota(jnp.int32, (ATTN_TILE, ATTN_TILE), 0)
    qry = lax.broadcasted_iota(jnp.int32, (ATTN_TILE, ATTN_TILE), 1)
    causal = key <= qry

    def body(n, carry):
        keep = jnp.logical_or(causal, n < i)
        for h in range(HEADS):
            s_ref[h] = scores_t(h, n)
        for h in range(HEADS):
            s = jnp.where(keep, s_ref[h], NEG)
            m = m_ref[h]
            m2 = jnp.maximum(m, jnp.max(s, axis=0, keepdims=True))
            a = jnp.exp(m - m2)
            p = jnp.exp(s - m2)
            l_ref[h] = a * l_ref[h] + jnp.sum(p, axis=0, keepdims=True)
            acc_ref[h] = a * acc_ref[h]
            p_ref[h] = p.astype(BF16)
            m_ref[h] = m2
        for h in range(HEADS):
            acc_ref[h] = acc_ref[h] + _dot(values_t(h, n), p_ref[h])
        return carry

    lax.fori_loop(0, i + 1, body, 0)


def _gated_heads_out(acc_ref, l_ref, g_ref, o_ref):
    o_t = jnp.concatenate([acc_ref[h] / l_ref[h] for h in range(HEADS)], axis=0)
    o_ref[...] = (jnp.transpose(o_t) * _silu(g_ref[...])).astype(o_ref.dtype)


def _flash_scratch():
    return [pltpu.VMEM((HEADS, 1, ATTN_TILE), F32), pltpu.VMEM((HEADS, 1, ATTN_TILE), F32),
            pltpu.VMEM((HEADS, V_DIM, ATTN_TILE), F32), pltpu.VMEM((HEADS, ATTN_TILE, ATTN_TILE), F32),
            pltpu.VMEM((HEADS, ATTN_TILE, ATTN_TILE), BF16)]


def _key_tile(n):
    return pl.ds(pl.multiple_of(n * ATTN_TILE, ATTN_TILE), ATTN_TILE)


def _moba_prompt_kernel(nb, topk, q_ref, k_ref, vt_ref, g_ref, o_ref,
                        kaug_ref, vtb_ref, bmh_ref, bml_ref, qaug_ref, *flash_refs):
    i = pl.program_id(1)
    lane = lax.broadcasted_iota(jnp.int32, (ATTN_TILE, LANES), 1)

    @pl.when(i == 0)
    def _prepare_sequence():
        for n in range(nb):
            vtb_ref[n] = vt_ref[:, n * MOBA_BLOCK:(n + 1) * MOBA_BLOCK].astype(BF16)
        means = [jnp.mean(k_ref[n * MOBA_BLOCK:(n + 1) * MOBA_BLOCK, :], axis=0, keepdims=True) if n < nb
                 else jnp.zeros((1, MOBA_W), F32) for n in range(8)]
        km = jnp.concatenate(means, axis=0)
        col = lax.broadcasted_iota(jnp.int32, (8, MOBA_W), 1)
        per_head = [jnp.where(col // MOBA_DH == h, km, 0.0) for h in range(HEADS)]
        bm = jnp.concatenate(per_head + per_head, axis=0)
        hi = bm.astype(BF16)
        bmh_ref[...] = hi
        bml_ref[...] = (bm - hi.astype(F32)).astype(BF16)
        for n in range(nb):
            for h in range(HEADS):
                p, half = divmod(h, 2)
                kp = k_ref[n * MOBA_BLOCK:(n + 1) * MOBA_BLOCK, p * LANES:(p + 1) * LANES]
                keep = (lane < MOBA_DH) if half == 0 else (lane >= MOBA_DH)
                hot = (MOBA_DH + 8 * h + n) if half == 0 else (8 * h + n)
                kaug_ref[h, n * MOBA_BLOCK:(n + 1) * MOBA_BLOCK, :] = jnp.where(
                    keep, kp, jnp.where(lane == hot, 1.0, 0.0)).astype(BF16)

    q = q_ref[...]
    gate = _dot_nt(q, bmh_ref[...]) + _dot_nt(q, bml_ref[...])
    nl = lane % 8
    valid = nl < i
    gm = jnp.where(valid, gate, NEG)
    rank = jnp.zeros(gm.shape, F32)
    for s in range(1, 8):
        below = pltpu.roll(gm, s, axis=1)
        above = pltpu.roll(gm, LANES - s, axis=1)
        rank = rank + jnp.where((nl >= s) & (below >= gm), 1.0, 0.0)
        rank = rank + jnp.where((nl + s < 8) & (above > gm), 1.0, 0.0)
    attend = (valid & (rank < topk)) | (nl == i)
    bias = jnp.where(attend, 0.0, NEG)

    for h in range(HEADS):
        p, half = divmod(h, 2)
        qp = q[:, p * LANES:(p + 1) * LANES].astype(F32)
        if half == 0:
            q_aug = jnp.where(lane < MOBA_DH, qp, jnp.where(lane // 8 == 8 + h, bias, 0.0))
        else:
            q_aug = jnp.where(lane >= MOBA_DH, qp, jnp.where(lane // 8 == h, bias, 0.0))
        qaug_ref[h] = q_aug.astype(BF16)

    def scores_t(h, n):
        return _dot_nt(kaug_ref[h, _key_tile(n), :], qaug_ref[h])

    def values_t(h, n):
        return vtb_ref[n, h * MOBA_DH:(h + 1) * MOBA_DH, :]

    _flash_heads(i, scores_t, values_t, *flash_refs)
    _gated_heads_out(flash_refs[2], flash_refs[1], g_ref, o_ref)


def _moba_prompt(qa, ka, vat, ga, batch, seq):
    nq = seq // ATTN_TILE
    nb = seq // MOBA_BLOCK
    tile = lambda w: pl.BlockSpec((ATTN_TILE, w), lambda b, i: (b * nq + i, 0))
    whole = lambda w: pl.BlockSpec((seq, w), lambda b, i: (b, 0))
    whole_t = pl.BlockSpec((None, MOBA_W, seq), lambda b, i: (b, 0, 0))
    return pl.pallas_call(
        functools.partial(_moba_prompt_kernel, nb, min(MOBA_TOPK, nb)),
        out_shape=jax.ShapeDtypeStruct((batch * seq, MOBA_W), BF16),
        grid=(batch, nq),
        in_specs=[tile(MOBA_W), whole(MOBA_W), whole_t, tile(MOBA_W)],
        out_specs=tile(MOBA_W),
        scratch_shapes=[pltpu.VMEM((HEADS, seq, LANES), BF16), pltpu.VMEM((nb, MOBA_W, MOBA_BLOCK), BF16),
                        pltpu.VMEM((LANES, MOBA_W), BF16), pltpu.VMEM((LANES, MOBA_W), BF16),
                        pltpu.VMEM((HEADS, ATTN_TILE, LANES), BF16)] + _flash_scratch(),
        compiler_params=pltpu.CompilerParams(dimension_semantics=("arbitrary", "arbitrary"),
                                             vmem_limit_bytes=VMEM_LIMIT),
        name="moba_prompt",
    )(qa, ka, vat, ga)


def _mla_prompt_kernel(nt, q_ref, k_ref, vt_ref, g_ref, o_ref, vtb_ref, *flash_refs):
    i = pl.program_id(1)

    @pl.when(i == 0)
    def _prepare_sequence():
        for n in range(nt):
            vtb_ref[n] = vt_ref[:, n * ATTN_TILE:(n + 1) * ATTN_TILE]

    def scores_t(h, n):
        return _dot_nt(k_ref[_key_tile(n), h * LANES:(h + 1) * LANES], q_ref[:, h * LANES:(h + 1) * LANES])

    def values_t(h, n):
        return vtb_ref[n, h * V_DIM:(h + 1) * V_DIM, :]

    _flash_heads(i, scores_t, values_t, *flash_refs)
    _gated_heads_out(flash_refs[2], flash_refs[1], g_ref, o_ref)


def _mla_prompt(qcat, kcat, vmt, gb, batch, seq):
    nq = seq // ATTN_TILE
    tile = lambda w: pl.BlockSpec((ATTN_TILE, w), lambda b, i: (b * nq + i, 0))
    whole = lambda w: pl.BlockSpec((seq, w), lambda b, i: (b, 0))
    whole_t = pl.BlockSpec((None, MLA_W, seq), lambda b, i: (b, 0, 0))
    return pl.pallas_call(
        functools.partial(_mla_prompt_kernel, nq),
        out_shape=jax.ShapeDtypeStruct((batch * seq, MLA_W), BF16),
        grid=(batch, nq),
        in_specs=[tile(HEADS * LANES), whole(HEADS * LANES), whole_t, tile(MLA_W)],
        out_specs=tile(MLA_W),
        scratch_shapes=[pltpu.VMEM((nq, MLA_W, ATTN_TILE), BF16)] + _flash_scratch(),
        compiler_params=pltpu.CompilerParams(dimension_semantics=("arbitrary", "arbitrary"),
                                             vmem_limit_bytes=VMEM_LIMIT),
        name="mla_prompt",
    )(qcat, kcat, vmt, gb)


def _merge_kernel(n_gates, x_ref, oa_ref, ob_ref, *refs):
    gate_refs = refs[:n_gates]
    g1_ref, g2_ref, wa_ref, wb_ref, wo_ref, y_ref = refs[n_gates:]
    oa, ob = oa_ref[...], ob_ref[...]
    if n_gates:
        oa = (oa * _silu(gate_refs[0][...])).astype(BF16)
        ob = (ob * _silu(gate_refs[1][...])).astype(BF16)
    a = _dot(oa, wa_ref[...])
    b = _dot(ob, wb_ref[...])
    hm = _sigmoid(g1_ref[...]) * a + _sigmoid(g2_ref[...]) * b
    y_ref[...] = x_ref[...] + _dot(hm.astype(BF16), wo_ref[...])


def _merge(x, oa, ob, branch_gates, g1, g2, wa, wb, wo):
    t = x.shape[0]
    tm = min(TOKEN_TILE, t)
    row = lambda w: pl.BlockSpec((tm, w), lambda i: (i, 0))
    return pl.pallas_call(
        functools.partial(_merge_kernel, len(branch_gates)),
        out_shape=jax.ShapeDtypeStruct((t, D_MODEL), F32),
        grid=(t // tm,),
        in_specs=[row(D_MODEL), row(MOBA_W), row(MLA_W)] + [row(g.shape[1]) for g in branch_gates] +
                 [row(D_MODEL), row(D_MODEL), _const_spec(wa.shape), _const_spec(wb.shape), _const_spec(wo.shape)],
        out_specs=row(D_MODEL),
        compiler_params=pltpu.CompilerParams(dimension_semantics=("arbitrary",), vmem_limit_bytes=VMEM_LIMIT),
        name="merge_sample" if branch_gates else "merge_prompt",
    )(x, oa, ob, *branch_gates, g1, g2, wa, wb, wo)


def _row_to_cols(row):
    eye = (lax.broadcasted_iota(jnp.int32, (MOBA_DH, MOBA_DH), 0) == lax.broadcasted_iota(jnp.int32, (MOBA_DH, MOBA_DH), 1))
    return [jnp.sum(jnp.where(eye, jnp.broadcast_to(row[:, h * MOBA_DH:(h + 1) * MOBA_DH], (MOBA_DH, MOBA_DH)), 0.0),
                    axis=1, keepdims=True) for h in range(HEADS)]


def _cols_to_row(cols):
    eye = (lax.broadcasted_iota(jnp.int32, (MOBA_DH, MOBA_DH), 0) == lax.broadcasted_iota(jnp.int32, (MOBA_DH, MOBA_DH), 1))
    return jnp.concatenate([jnp.sum(jnp.where(eye, jnp.broadcast_to(c, (MOBA_DH, MOBA_DH)), 0.0), axis=0, keepdims=True)
                            for c in cols], axis=1)


def _moba_select_kernel(ppc, pt_ref, q_ref, *refs):
    page_refs = refs[:ppc]
    sel_ref, qb_ref, gate_ref = refs[ppc:]
    c = pl.program_id(1)
    lane = lax.broadcasted_iota(jnp.int32, (HEADS, LANES), 1)
    low = jnp.float32(-3.0e38)

    @pl.when(c == 0)
    def _start_sequence():
        for h, qc in enumerate(_row_to_cols(q_ref[...])):
            qb_ref[h] = jnp.broadcast_to(qc, (MOBA_DH, LANES))
        gate_ref[...] = jnp.full((HEADS, LANES), low, F32)

    qb = qb_ref[...]
    gate = gate_ref[...]
    blocks_per_step = ppc // 2
    for j in range(blocks_per_step):
        ksum = page_refs[2 * j][...] + page_refs[2 * j + 1][...]
        part = jnp.sum(ksum * qb, axis=1)
        g = jnp.sum(part, axis=1, keepdims=True) * (1.0 / MOBA_BLOCK)
        gate = jnp.where(lane == c * blocks_per_step + j, g, gate)
    gate_ref[...] = gate

    @pl.when(c == pl.num_programs(1) - 1)
    def _pick_blocks():
        gw = gate
        out = jnp.zeros((HEADS, LANES), jnp.int32)
        for r in range(MOBA_TOPK):
            mx = jnp.max(gw, axis=1, keepdims=True)
            idx = jnp.min(jnp.where(gw == mx, lane, LANES), axis=1, keepdims=True)
            out = jnp.where(lane == r, idx, out)
            gw = jnp.where(lane == idx, low, gw)
        sel_ref[...] = out


def _moba_select(page_table, q_row, kt_pages):
    batch, n_pages = page_table.shape
    ppc = min(PAGES_PER_STEP, n_pages)
    page_spec = lambda k: pl.BlockSpec((None, HEADS, MOBA_DH, PAGE), lambda b, c, pt: (pt[b, c * ppc + k], 0, 0, 0))
    grid_spec = pltpu.PrefetchScalarGridSpec(
        num_scalar_prefetch=1,
        grid=(batch, n_pages // ppc),
        in_specs=[pl.BlockSpec((None, 1, MOBA_W), lambda b, c, pt: (b, 0, 0))] + [page_spec(k) for k in range(ppc)],
        out_specs=pl.BlockSpec((None, HEADS, LANES), lambda b, c, pt: (b, 0, 0)),
        scratch_shapes=[pltpu.VMEM((HEADS, MOBA_DH, LANES), F32), pltpu.VMEM((HEADS, LANES), F32)],
    )
    return pl.pallas_call(
        functools.partial(_moba_select_kernel, ppc),
        out_shape=jax.ShapeDtypeStruct((batch, HEADS, LANES), jnp.int32),
        grid_spec=grid_spec,
        compiler_params=pltpu.CompilerParams(dimension_semantics=("arbitrary", "arbitrary"),
                                             vmem_limit_bytes=VMEM_LIMIT),
        name="moba_select",
    )(page_table, q_row, *([kt_pages] * ppc))


SEL_PAGES = 2 * MOBA_TOPK


def _moba_attend_kernel(pt_ref, sel_ref, q_ref, kn_ref, vn_ref, kt_hbm, vt_hbm, o_ref, kbuf, vbuf, sem):
    b = pl.program_id(0)
    slot = b % 2

    def copies(seq, slot):
        out = []
        for h in range(HEADS):
            for r in range(MOBA_TOPK):
                blk = sel_ref[seq, h * MOBA_TOPK + r]
                for pg in range(2):
                    page = pt_ref[seq, 2 * blk + pg]
                    out.append(pltpu.make_async_copy(kt_hbm.at[page, h], kbuf.at[slot, h, 2 * r + pg], sem.at[0, slot]))
                    out.append(pltpu.make_async_copy(vt_hbm.at[page, h], vbuf.at[slot, h, 2 * r + pg], sem.at[1, slot]))
        return out

    @pl.when(b == 0)
    def _prime():
        for cp in copies(0, 0):
            cp.start()

    @pl.when(b + 1 < pl.num_programs(0))
    def _prefetch_next():
        for cp in copies(b + 1, 1 - slot):
            cp.start()

    for cp in copies(b, slot):
        cp.wait()

    q_cols = _row_to_cols(q_ref[...])
    k_cols = _row_to_cols(kn_ref[...])
    v_cols = _row_to_cols(vn_ref[...])
    outs = []
    for h in range(HEADS):
        qb = jnp.broadcast_to(q_cols[h], (MOBA_DH, LANES))
        s = jnp.concatenate([jnp.sum(kbuf[slot, h, j] * qb, axis=0, keepdims=True) for j in range(SEL_PAGES)], axis=0)
        sn = jnp.sum(q_cols[h] * k_cols[h], axis=0, keepdims=True)
        m = jnp.maximum(jnp.max(jnp.max(s, axis=1, keepdims=True), axis=0, keepdims=True), sn)
        p = jnp.exp(s - m)
        pn = jnp.exp(sn - m)
        l = jnp.sum(jnp.sum(p, axis=1, keepdims=True), axis=0, keepdims=True) + pn
        acc = jnp.zeros((MOBA_DH, LANES), F32)
        for j in range(SEL_PAGES):
            acc = acc + vbuf[slot, h, j] * p[j:j + 1, :]
        outs.append((jnp.sum(acc, axis=1, keepdims=True) + pn * v_cols[h]) / l)
    o_ref[...] = _cols_to_row(outs)


def _moba_attend(page_table, sel, q_row, k_row, v_row, kt_pages, vt_pages):
    batch = page_table.shape[0]
    row = lambda: pl.BlockSpec((None, 1, MOBA_W), lambda b, pt, sl: (b, 0, 0))
    grid_spec = pltpu.PrefetchScalarGridSpec(
        num_scalar_prefetch=2,
        grid=(batch,),
        in_specs=[row(), row(), row(), pl.BlockSpec(memory_space=pl.ANY), pl.BlockSpec(memory_space=pl.ANY)],
        out_specs=row(),
        scratch_shapes=[pltpu.VMEM((2, HEADS, SEL_PAGES, MOBA_DH, PAGE), F32),
                        pltpu.VMEM((2, HEADS, SEL_PAGES, MOBA_DH, PAGE), F32),
                        pltpu.SemaphoreType.DMA((2, 2))],
    )
    return pl.pallas_call(
        _moba_attend_kernel,
        out_shape=jax.ShapeDtypeStruct((batch, 1, MOBA_W), F32),
        grid_spec=grid_spec,
        compiler_params=pltpu.CompilerParams(dimension_semantics=("arbitrary",), vmem_limit_bytes=VMEM_LIMIT),
        name="moba_attend",
    )(page_table, sel, q_row, k_row, v_row, kt_pages, vt_pages)


def _mla_decode_kernel(ppc, nch, pt_ref, qn_ref, qr_ref, cnew_ref, pnew_ref, wukt_ref, wuv_ref, gkn_ref, gkr_ref,
                       cos_ref, sin_ref, cosn_ref, sinn_ref, ckv_hbm, kpe_hbm, o_ref,
                       cbuf, pbuf, sem, a_ref, m_ref, l_ref, acc_ref):
    b = pl.program_id(0)
    c = pl.program_id(1)
    step = b * nch + c
    slot = step % 2
    lhs_rows = HEADS * NOPE
    tk = ppc * PAGE

    def copies(seq, chunk, slot):
        out = []
        for k in range(ppc):
            page = pt_ref[seq, chunk * ppc + k]
            out.append(pltpu.make_async_copy(ckv_hbm.at[page], cbuf.at[slot, k], sem.at[0, slot]))
            out.append(pltpu.make_async_copy(kpe_hbm.at[page], pbuf.at[slot, k], sem.at[1, slot]))
        return out

    @pl.when(step == 0)
    def _prime():
        for cp in copies(0, 0, 0):
            cp.start()
        a_ref[0:lhs_rows, :] = wukt_ref[...]

    last_chunk = c == nch - 1

    @pl.when(step + 1 < pl.num_programs(0) * nch)
    def _prefetch_next():
        for cp in copies(jnp.where(last_chunk, b + 1, b), jnp.where(last_chunk, 0, c + 1), 1 - slot):
            cp.start()

    @pl.when(c == 0)
    def _start_sequence():
        rowh = lax.broadcasted_iota(jnp.int32, (HEADS, MLA_W), 0)
        colh = lax.broadcasted_iota(jnp.int32, (HEADS, MLA_W), 1)
        qg = jnp.where(colh // NOPE == rowh, qn_ref[...] * gkn_ref[...], 0.0).astype(BF16)
        qlat = _dot(qg, wukt_ref[...])
        a_ref[lhs_rows:lhs_rows + 16, :] = jnp.concatenate([qlat, jnp.zeros_like(qlat)], axis=0).astype(BF16)
        m_ref[...] = jnp.full(m_ref.shape, NEG, F32)
        l_ref[...] = jnp.zeros(l_ref.shape, F32)
        acc_ref[...] = jnp.zeros(acc_ref.shape, F32)

    for cp in copies(b, c, slot):
        cp.wait()

    qr = qr_ref[...].astype(BF16)
    gkr = gkr_ref[...]

    def attend(cb, kpt, cos, sin, key_ok):
        kt = _dot_nt(a_ref[...], cb)
        n_keys = cb.shape[0]
        kn = kt[0:lhs_rows, :]
        ssn = jnp.sum((kn * kn).reshape(HEADS, NOPE, n_keys), axis=1)
        sn = kt[lhs_rows:lhs_rows + HEADS, :]
        n2 = jnp.sum(kpt * kpt, axis=0, keepdims=True)
        y = kpt * gkr
        y1, y2 = y[0:ROPE // 2, :], y[ROPE // 2:ROPE, :]
        kr = jnp.concatenate([y1 * cos - y2 * sin, y2 * cos + y1 * sin], axis=0)
        sr = _dot(qr, kr.astype(BF16))
        s = (sn + sr) * lax.rsqrt((ssn + n2) * (1.0 / QK_DIM) + EPS)
        if key_ok is not None:
            s = jnp.where(key_ok, s, NEG)
        m = m_ref[...]
        m2 = jnp.maximum(m, jnp.max(s, axis=1, keepdims=True))
        a = jnp.exp(m - m2)
        p = jnp.exp(s - m2)
        l_ref[...] = a * l_ref[...] + jnp.sum(p, axis=1, keepdims=True)
        acc_ref[...] = a * acc_ref[...] + _dot(p.astype(BF16), cb)
        m_ref[...] = m2

    cb = cbuf[slot].reshape(tk, KV_LORA).astype(BF16)
    kpt = jnp.concatenate([pbuf[slot, k] for k in range(ppc)], axis=1)
    attend(cb, kpt, cos_ref[c], sin_ref[c], None)

    @pl.when(last_chunk)
    def _finish_sequence():
        rowi = lax.broadcasted_iota(jnp.int32, (PAGE, KV_LORA), 0)
        lanei = lax.broadcasted_iota(jnp.int32, (ROPE, PAGE), 1)
        cb = jnp.where(rowi == 0, jnp.broadcast_to(cnew_ref[...], (PAGE, KV_LORA)), 0.0).astype(BF16)
        kpt = jnp.where(lanei == 0, jnp.broadcast_to(pnew_ref[...], (ROPE, PAGE)), 0.0)
        key_ok = lax.broadcasted_iota(jnp.int32, (HEADS, PAGE), 1) == 0
        attend(cb, kpt, cosn_ref[...], sinn_ref[...], key_ok)
        ol = (acc_ref[...] / l_ref[...]).astype(BF16)
        res = _dot(ol, wuv_ref[...])
        rowh = lax.broadcasted_iota(jnp.int32, (HEADS, MLA_W), 0)
        colh = lax.broadcasted_iota(jnp.int32, (HEADS, MLA_W), 1)
        o_ref[...] = jnp.sum(jnp.where(colh // V_DIM == rowh, res, 0.0), axis=0, keepdims=True)


def _mla_decode(page_table, qn, qr, c_new, p_new, wukt, wuv, gkn, gkr_col, cos_t, sin_t, cos_n, sin_n,
                ckv_pages, kpet_pages):
    batch, n_pages = page_table.shape
    ppc = min(PAGES_PER_STEP, n_pages)
    nch = n_pages // ppc
    seq_spec = lambda shape: pl.BlockSpec((None,) + shape, lambda b, c, pt: (b,) + (0,) * len(shape))
    const = lambda arr: pl.BlockSpec(arr.shape, lambda b, c, pt: (0,) * arr.ndim)
    grid_spec = pltpu.PrefetchScalarGridSpec(
        num_scalar_prefetch=1,
        grid=(batch, nch),
        in_specs=[seq_spec((1, MLA_W)), seq_spec((HEADS, ROPE)), seq_spec((1, KV_LORA)), seq_spec((ROPE, 1)),
                  const(wukt), const(wuv), const(gkn), const(gkr_col), const(cos_t), const(sin_t), const(cos_n),
                  const(sin_n), pl.BlockSpec(memory_space=pl.ANY), pl.BlockSpec(memory_space=pl.ANY)],
        out_specs=seq_spec((1, MLA_W)),
        scratch_shapes=[pltpu.VMEM((2, ppc, PAGE, KV_LORA), F32), pltpu.VMEM((2, ppc, ROPE, PAGE), F32),
                        pltpu.SemaphoreType.DMA((2, 2)),
                        pltpu.VMEM((HEADS * NOPE + 16, KV_LORA), BF16), pltpu.VMEM((HEADS, 1), F32),
                        pltpu.VMEM((HEADS, 1), F32), pltpu.VMEM((HEADS, KV_LORA), F32)],
    )
    return pl.pallas_call(
        functools.partial(_mla_decode_kernel, ppc, nch),
        out_shape=jax.ShapeDtypeStruct((batch, 1, MLA_W), F32),
        grid_spec=grid_spec,
        compiler_params=pltpu.CompilerParams(dimension_semantics=("arbitrary", "arbitrary"),
                                             vmem_limit_bytes=VMEM_LIMIT),
        name="mla_decode",
    )(page_table, qn, qr, c_new, p_new, wukt, wuv, gkn, gkr_col, cos_t, sin_t, cos_n, sin_n, ckv_pages, kpet_pages)


def _rope_angles(pos, half):
    inv = np.power(np.float64(ROPE_THETA), -np.arange(half, dtype=np.float64) / half)
    return np.asarray(pos, np.float64)[:, None] * inv[None, :]


def _row_tables(pos, half):
    ang = _rope_angles(pos, half)
    k = np.arange(LANES) % (2 * half)
    cos = np.cos(ang)[:, k % half]
    sin = np.sin(ang)[:, k % half] * np.where(k < half, -1.0, 1.0).astype(np.float32)[None, :]
    return jnp.asarray(cos, F32), jnp.asarray(sin, F32)


def _block_ones(rows_per_group, groups, cols):
    m = np.zeros((groups * rows_per_group, cols), np.float32)
    m[np.arange(groups * rows_per_group), np.arange(groups * rows_per_group) // rows_per_group] = 1.0
    return m


def kernel(x_prompt, x_sample, cache_moba_k, cache_moba_v, cache_mla_ckv, cache_mla_kpe, page_table, norm_gain, w_in, moba_q_gain, moba_k_gain, mla_q_lat_gain, w_uq, mla_q_gain, mla_kv_lat_gain, w_uk, w_uv, mla_k_gain, w_branch_a, w_branch_b, w_out):
    batch, seq, _ = x_prompt.shape
    dec_batch, dec_seq, _ = x_sample.shape
    depth = norm_gain.shape[0]
    assert depth == 1 and dec_seq == 1
    n_pages = page_table.shape[1]
    past = n_pages * PAGE
    assert seq % ATTN_TILE == 0 and seq // MOBA_BLOCK <= 8 and past % MOBA_BLOCK == 0
    assert past // MOBA_BLOCK >= MOBA_TOPK and n_pages % min(PAGES_PER_STEP, n_pages) == 0

    w = w_in[0]
    bf = lambda a: a.astype(BF16)
    o = 4 * MOBA_W
    wqk, wv, wg = bf(w[:, 0:2 * MOBA_W]), bf(w[:, 2 * MOBA_W:3 * MOBA_W]), bf(w[:, 3 * MOBA_W:o])
    w2 = bf(jnp.pad(w[:, o:o + Q_LORA + KV_LORA + ROPE], ((0, 0), (0, LANES - ROPE))))
    o += Q_LORA + KV_LORA + ROPE
    w3 = bf(w[:, o:o + MLA_W + 2 * D_MODEL])
    wuqn = bf(w_uq[0][:, :, :NOPE].reshape(Q_LORA, HEADS * NOPE))
    wuqr = bf(w_uq[0][:, :, NOPE:].reshape(Q_LORA, HEADS * ROPE))
    wuk = bf(w_uk[0].reshape(KV_LORA, HEADS * NOPE))
    wuv = bf(w_uv[0].reshape(KV_LORA, MLA_W))
    tile8 = lambda g: jnp.tile(g, HEADS)[None, :].astype(F32)
    gains = [tile8(moba_q_gain[0]), tile8(moba_k_gain[0]), mla_q_lat_gain[0][None, :], mla_kv_lat_gain[0][None, :],
             tile8(mla_q_gain[0][:NOPE]), tile8(mla_q_gain[0][NOPE:]), tile8(mla_k_gain[0][:NOPE]),
             tile8(mla_k_gain[0][NOPE:])]
    place = lambda width, off: np.eye(HEADS * LANES, dtype=np.float32)[
        (np.arange(HEADS * width) // width) * LANES + off + np.arange(HEADS * width) % width]
    helpers = [
        jnp.asarray(_block_ones(MOBA_DH, HEADS, HEADS) @ _block_ones(MOBA_DH, HEADS, HEADS).T, BF16),
        jnp.asarray(_block_ones(NOPE, HEADS, LANES), BF16),
        jnp.asarray(_block_ones(ROPE, HEADS, LANES), BF16),
        jnp.asarray(_block_ones(NOPE, HEADS, LANES).T, BF16),
        jnp.asarray(_block_ones(ROPE, HEADS, LANES).T, BF16),
        jnp.asarray(np.concatenate([np.tile(np.eye(ROPE, dtype=np.float32), (1, HEADS)),
                                    np.zeros((LANES - ROPE, HEADS * ROPE), np.float32)], axis=0), BF16),
        jnp.asarray(place(NOPE, 0), BF16),
        jnp.asarray(place(ROPE, NOPE), BF16),
    ]
    consts_p = [norm_gain[0][None, :], wqk, wv.T, wg, w2, w3, wuqn, wuqr, wuk, wuv.T] + gains + helpers
    consts_s = [norm_gain[0][None, :], wqk, wv, wg, w2, w3, wuqn, wuqr, wuk, wuv] + gains + helpers
    assert len(consts_p) - len(helpers) == N_FRONT_WEIGHTS
    pos_p = np.arange(seq)
    pos_s = np.full((dec_batch,), past)
    tables_p = _row_tables(pos_p, MOBA_DH // 2) + _row_tables(pos_p, ROPE // 2)
    tables_s = _row_tables(pos_s, MOBA_DH // 2) + _row_tables(pos_s, ROPE // 2)
    wa = bf(w_branch_a[0])
    wb = bf(w_branch_b[0])
    wo = bf(w_out[0])

    xp = x_prompt.reshape(batch * seq, D_MODEL)
    (qa, ka, vat, ga, ckv, kpe, gb, g1, g2, qcat, kcat, vmt) = _front(xp, consts_p, tables_p, True, seq)
    oa = _moba_prompt(qa, ka, vat, ga, batch, seq)
    ob = _mla_prompt(qcat, kcat, vmt, gb, batch, seq)
    y_prompt = _merge(xp, oa, ob, (), g1, g2, wa, wb, wo).reshape(batch, seq, D_MODEL)

    xs = x_sample.reshape(dec_batch, D_MODEL)
    (qa_s, ka_s, va_s, ga_s, ckv_s, kpe_s, gb_s, g1_s, g2_s, qn_s, qr_s) = _front(xs, consts_s, tables_s, False, 1)
    kt_pages = jnp.transpose(cache_moba_k[0], (0, 2, 3, 1))
    vt_pages = jnp.transpose(cache_moba_v[0], (0, 2, 3, 1))
    ckv_pages = cache_mla_ckv[0]
    kpet_pages = jnp.transpose(cache_mla_kpe[0], (0, 2, 1))
    row3 = lambda a: a.reshape(dec_batch, 1, MOBA_W)
    sel = _moba_select(page_table, row3(qa_s), kt_pages)[:, :, :MOBA_TOPK].reshape(dec_batch, HEADS * MOBA_TOPK)
    oa_s = _moba_attend(page_table, sel, row3(qa_s), row3(ka_s), row3(va_s), kt_pages, vt_pages).reshape(dec_batch, MOBA_W)

    ppc = min(PAGES_PER_STEP, n_pages)
    ang_past = _rope_angles(np.arange(past), ROPE // 2).T
    ang_past = ang_past.reshape(ROPE // 2, n_pages // ppc, ppc * PAGE).transpose(1, 0, 2)
    ang_new = np.broadcast_to(_rope_angles(np.array([past]), ROPE // 2).T, (ROPE // 2, PAGE))
    wukt = jnp.transpose(w_uk[0].reshape(KV_LORA, HEADS * NOPE)).astype(BF16)
    ob_s = _mla_decode(
        page_table, qn_s.reshape(dec_batch, 1, MLA_W), qr_s.reshape(dec_batch, HEADS, ROPE),
        ckv_s.reshape(dec_batch, 1, KV_LORA), kpe_s.reshape(dec_batch, ROPE, 1),
        wukt, wuv, tile8(mla_k_gain[0][:NOPE]), mla_k_gain[0][NOPE:].reshape(ROPE, 1),
        jnp.asarray(np.cos(ang_past), F32), jnp.asarray(np.sin(ang_past), F32),
        jnp.asarray(np.cos(ang_new), F32), jnp.asarray(np.sin(ang_new), F32),
        ckv_pages, kpet_pages).reshape(dec_batch, MLA_W)
    y_sample = _merge(xs, oa_s, ob_s, (ga_s, gb_s), g1_s, g2_s, wa, wb, wo).reshape(dec_batch, 1, D_MODEL)

    p5 = lambda a, w: a.reshape(1, batch, seq, HEADS, w)
    s5 = lambda a, w: a.reshape(1, dec_batch, 1, HEADS, w)
    return (y_prompt, y_sample,
            p5(ka, MOBA_DH), vat.reshape(1, batch, HEADS, MOBA_DH, seq).transpose(0, 1, 4, 2, 3), ckv.reshape(1, batch, seq, KV_LORA), kpe.reshape(1, batch, seq, ROPE),
            s5(ka_s, MOBA_DH), s5(va_s, MOBA_DH), ckv_s.reshape(1, dec_batch, 1, KV_LORA),
            kpe_s.reshape(1, dec_batch, 1, ROPE))
```

```python
import functools

import jax
import jax.numpy as jnp
import numpy as np
from jax import lax
from jax.experimental import pallas as pl
from jax.experimental.pallas import tpu as pltpu

F32 = jnp.float32
BF16 = jnp.bfloat16

D_MODEL = 1024
HEADS = 8
MOBA_DH = 64
MOBA_W = HEADS * MOBA_DH
MOBA_BLOCK = 256
MOBA_TOPK = 3
NOPE = 64
ROPE = 32
QK_DIM = NOPE + ROPE
V_DIM = 64
MLA_W = HEADS * V_DIM
Q_LORA = 384
KV_LORA = 256
PAGE = 128
ROPE_THETA = 10000.0
EPS = 1e-6
NEG = -1e30

LANES = 128
V7X_VMEM_BYTES = 64 * 1024 * 1024
VMEM_LIMIT = V7X_VMEM_BYTES - 8 * 1024 * 1024

TOKEN_TILE = 256
ATTN_TILE = MOBA_BLOCK
PAGES_PER_STEP = 16
MLA_PAGES_PER_STEP = 32
MLA_SCORE_PAGES = 16


def _dot(a, b):
    return jnp.dot(a, b, preferred_element_type=F32)


def _dot_nt(a, b):
    return lax.dot_general(a, b, (((1,), (1,)), ((), ())), preferred_element_type=F32)


def _dot_split(x, w):
    hi = x.astype(BF16)
    lo = (x - hi.astype(F32)).astype(BF16)
    return _dot(hi, w) + _dot(lo, w)


def _sigmoid(x):
    return 1.0 / (1.0 + jnp.exp(-x))


def _silu(x):
    return x * _sigmoid(x)


def _rope_rows(y, cos, sin_signed, half):
    lane = lax.broadcasted_iota(jnp.int32, (y.shape[0], LANES), 1)
    first = (lane % (2 * half)) < half
    outs = []
    for c in range(y.shape[1] // LANES):
        yc = y[:, c * LANES:(c + 1) * LANES]
        up = pltpu.roll(yc, LANES - half, axis=1)
        dn = pltpu.roll(yc, half, axis=1)
        outs.append(yc * cos + jnp.where(first, up, dn) * sin_signed)
    return jnp.concatenate(outs, axis=1)


def _front_kernel(prompt, x_ref, ng_ref, wqk_ref, wv_ref, wg_ref, w2_ref, w3_ref, wuqn_ref, wuqr_ref, wuk_ref, wuv_ref,
                  gq_ref, gk_ref, gcq_ref, gckv_ref, gqn_ref, gqr_ref, gkn_ref, gkr_ref,
                  cos64_ref, sin64_ref, cos32_ref, sin32_ref,
                  g64_ref, s64_ref, s32_ref, e64_ref, e32_ref, t32_ref, pn_ref, pr_ref,
                  qa_ref, ka_ref, v_ref, ga_ref, ckv_ref, kpe_ref, gb_ref, g1_ref, g2_ref, *mode_refs):
    x = x_ref[...]
    h = (x * lax.rsqrt(jnp.mean(x * x, axis=-1, keepdims=True) + EPS) * ng_ref[...]).astype(BF16)

    z1 = _dot(h, wqk_ref[...])
    g64 = g64_ref[...]
    cos64, sin64 = cos64_ref[...], sin64_ref[...]

    def head_norm64(z, gain):
        ss = _dot_split(z * z, g64)
        return z * lax.rsqrt(ss * (1.0 / MOBA_DH) + EPS) * gain

    qa = _rope_rows(head_norm64(z1[:, 0:MOBA_W], gq_ref[...]), cos64, sin64, MOBA_DH // 2)
    qa_ref[...] = (qa * MOBA_DH ** -0.5).astype(qa_ref.dtype)
    ka = _rope_rows(head_norm64(z1[:, MOBA_W:2 * MOBA_W], gk_ref[...]), cos64, sin64, MOBA_DH // 2)
    if prompt:
        ka_ref[...] = jnp.transpose(ka)
        v_ref[...] = _dot_nt(wv_ref[...], h)
    else:
        ka_ref[...] = ka
        v_ref[...] = _dot(h, wv_ref[...])
    ga_ref[...] = _dot(h, wg_ref[...])

    z2 = _dot(h, w2_ref[...])
    cq = z2[:, 0:Q_LORA]
    cqn = (cq * lax.rsqrt(jnp.mean(cq * cq, axis=-1, keepdims=True) + EPS) * gcq_ref[...]).astype(BF16)
    qn = _dot(cqn, wuqn_ref[...])
    qr = _dot(cqn, wuqr_ref[...])
    s64, s32, e64, e32 = s64_ref[...], s32_ref[...], e64_ref[...], e32_ref[...]
    ss = _dot_split(qn * qn, s64) + _dot_split(qr * qr, s32)
    r = lax.rsqrt(ss * (1.0 / QK_DIM) + EPS)
    scale = QK_DIM ** -0.5
    cos32, sin32 = cos32_ref[...], sin32_ref[...]
    qn_n = qn * _dot_split(r, e64) * (gqn_ref[...] * scale)
    qr_n = _rope_rows(qr * _dot_split(r, e32) * (gqr_ref[...] * scale), cos32, sin32, ROPE // 2)

    ckv_raw = z2[:, Q_LORA:Q_LORA + KV_LORA]
    ckv = ckv_raw * lax.rsqrt(jnp.mean(ckv_raw * ckv_raw, axis=-1, keepdims=True) + EPS) * gckv_ref[...]
    ckv_ref[...] = ckv
    kpe_p = z2[:, Q_LORA + KV_LORA:Q_LORA + KV_LORA + LANES]
    kpe_ref[...] = kpe_p[:, 0:ROPE]

    z3 = _dot(h, w3_ref[...])
    gb_ref[...] = z3[:, 0:MLA_W]
    g1_ref[...] = z3[:, MLA_W:MLA_W + D_MODEL]
    g2_ref[...] = z3[:, MLA_W + D_MODEL:MLA_W + 2 * D_MODEL]

    if not prompt:
        qn_ref, qr_ref = mode_refs
        qn_ref[...] = qn_n
        qr_ref[...] = qr_n
        return
    qcat_ref, kcat_ref, vmt_ref = mode_refs
    pn, pr = pn_ref[...], pr_ref[...]
    qcat_ref[...] = (_dot(qn_n.astype(BF16), pn) + _dot(qr_n.astype(BF16), pr)).astype(BF16)
    cb = ckv.astype(BF16)
    kn = _dot(cb, wuk_ref[...])
    ssk = _dot_split(kn * kn, s64) + jnp.sum(kpe_p * kpe_p, axis=-1, keepdims=True)
    rk = lax.rsqrt(ssk * (1.0 / QK_DIM) + EPS)
    kn_n = (kn * _dot_split(rk, e64) * gkn_ref[...]).astype(BF16)
    kt = _dot_split(kpe_p, t32_ref[...]) * gkr_ref[...]
    kr_n = (_rope_rows(kt, cos32, sin32, ROPE // 2) * _dot_split(rk, e32)).astype(BF16)
    kcat_ref[...] = (_dot(kn_n, pn) + _dot(kr_n, pr)).astype(BF16)
    vmt_ref[...] = _dot_nt(wuv_ref[...], cb).astype(BF16)


def _const_spec(shape):
    nd = len(shape)
    return pl.BlockSpec(shape, lambda *_: (0,) * nd, pipeline_mode=pl.Buffered(1))


N_FRONT_WEIGHTS = 18


def _front(x, consts, tables, prompt, seq):
    t = x.shape[0]
    tm = min(TOKEN_TILE, t)
    n_tab = tables[0].shape[0] // tm
    row = lambda w: pl.BlockSpec((tm, w), lambda i: (i, 0))
    tab = lambda: pl.BlockSpec((tm, LANES), lambda i: (i % n_tab, 0))
    in_specs = [row(D_MODEL)] + [_const_spec(c.shape) for c in consts[:N_FRONT_WEIGHTS]] + [tab() for _ in range(4)] + \
               [_const_spec(c.shape) for c in consts[N_FRONT_WEIGHTS:]]
    q_dtype = BF16 if prompt else F32
    rows = lambda w, d: (jax.ShapeDtypeStruct((t, w), d), row(w))
    if prompt:
        nq = seq // tm
        cols = lambda d: (jax.ShapeDtypeStruct((t // seq, MOBA_W, seq), d),
                          pl.BlockSpec((None, MOBA_W, tm), lambda i: (i // nq, 0, i % nq)))
        kv_out = cols(F32)
        mode_outs = [rows(HEADS * LANES, BF16), rows(HEADS * LANES, BF16), cols(BF16)]
    else:
        kv_out = rows(MOBA_W, F32)
        mode_outs = [rows(MLA_W, F32), rows(HEADS * ROPE, F32)]
    outs = [rows(MOBA_W, q_dtype), kv_out, kv_out, rows(MOBA_W, F32), rows(KV_LORA, F32), rows(ROPE, F32),
            rows(MLA_W, F32), rows(D_MODEL, F32), rows(D_MODEL, F32)] + mode_outs
    return pl.pallas_call(
        functools.partial(_front_kernel, prompt),
        out_shape=[o[0] for o in outs],
        grid=(t // tm,),
        in_specs=in_specs,
        out_specs=[o[1] for o in outs],
        compiler_params=pltpu.CompilerParams(dimension_semantics=("arbitrary",), vmem_limit_bytes=VMEM_LIMIT),
        name="front_prompt" if prompt else "front_sample",
    )(x, *consts[:N_FRONT_WEIGHTS], *tables, *consts[N_FRONT_WEIGHTS:])


def _flash_heads(i, scores_t, values_t, m_ref, l_ref, acc_ref, s_ref, p_ref):
    m_ref[...] = jnp.full(m_ref.shape, NEG, F32)
    l_ref[...] = jnp.zeros(l_ref.shape, F32)
    acc_ref[...] = jnp.zeros(acc_ref.shape, F32)
    key = lax.broadcasted_iota(jnp.int32, (ATTN_TILE, ATTN_TILE), 0)
    qry = lax.broadcasted_iota(jnp.int32, (ATTN_TILE, ATTN_TILE), 1)
    causal = key <= qry

    def body(n, carry):
        keep = jnp.logical_or(causal, n < i)
        for h in range(HEADS):
            s_ref[h] = scores_t(h, n)
        for h in range(HEADS):
            s = jnp.where(keep, s_ref[h], NEG)
            m = m_ref[h]
            m2 = jnp.maximum(m, jnp.max(s, axis=0, keepdims=True))
            a = jnp.exp(m - m2)
            p = jnp.exp(s - m2)
            l_ref[h] = a * l_ref[h] + jnp.sum(p, axis=0, keepdims=True)
            acc_ref[h] = a * acc_ref[h]
            p_ref[h] = p.astype(BF16)
            m_ref[h] = m2
        for h in range(HEADS):
            acc_ref[h] = acc_ref[h] + _dot(values_t(h, n), p_ref[h])
        return carry

    lax.fori_loop(0, i + 1, body, 0)


def _gated_heads_out(acc_ref, l_ref, g_ref, o_ref):
    o_t = jnp.concatenate([acc_ref[h] / l_ref[h] for h in range(HEADS)], axis=0)
    o_ref[...] = (jnp.transpose(o_t) * _silu(g_ref[...])).astype(o_ref.dtype)


def _flash_scratch():
    return [pltpu.VMEM((HEADS, 1, ATTN_TILE), F32), pltpu.VMEM((HEADS, 1, ATTN_TILE), F32),
            pltpu.VMEM((HEADS, V_DIM, ATTN_TILE), F32), pltpu.VMEM((HEADS, ATTN_TILE, ATTN_TILE), F32),
            pltpu.VMEM((HEADS, ATTN_TILE, ATTN_TILE), BF16)]


def _key_tile(n):
    return pl.ds(pl.multiple_of(n * ATTN_TILE, ATTN_TILE), ATTN_TILE)


def _moba_prompt_kernel(nb, topk, q_ref, kt_ref, vt_ref, g_ref, o_ref,
                        kaug_ref, vtb_ref, bmh_ref, bml_ref, qaug_ref, *flash_refs):
    i = pl.program_id(1)
    lane = lax.broadcasted_iota(jnp.int32, (ATTN_TILE, LANES), 1)

    @pl.when(i == 0)
    def _prepare_sequence():
        for n in range(nb):
            vtb_ref[n] = vt_ref[:, n * MOBA_BLOCK:(n + 1) * MOBA_BLOCK].astype(BF16)
        k_blocks = [jnp.transpose(kt_ref[:, n * MOBA_BLOCK:(n + 1) * MOBA_BLOCK]) for n in range(nb)]
        means = [jnp.mean(k_blocks[n], axis=0, keepdims=True) if n < nb
                 else jnp.zeros((1, MOBA_W), F32) for n in range(8)]
        km = jnp.concatenate(means, axis=0)
        col = lax.broadcasted_iota(jnp.int32, (8, MOBA_W), 1)
        per_head = [jnp.where(col // MOBA_DH == h, km, 0.0) for h in range(HEADS)]
        bm = jnp.concatenate(per_head + per_head, axis=0)
        hi = bm.astype(BF16)
        bmh_ref[...] = hi
        bml_ref[...] = (bm - hi.astype(F32)).astype(BF16)
        for n in range(nb):
            for h in range(HEADS):
                p, half = divmod(h, 2)
                kp = k_blocks[n][:, p * LANES:(p + 1) * LANES]
                keep = (lane < MOBA_DH) if half == 0 else (lane >= MOBA_DH)
                hot = (MOBA_DH + 8 * h + n) if half == 0 else (8 * h + n)
                kaug_ref[h, n * MOBA_BLOCK:(n + 1) * MOBA_BLOCK, :] = jnp.where(
                    keep, kp, jnp.where(lane == hot, 1.0, 0.0)).astype(BF16)

    q = q_ref[...]
    gate = _dot_nt(q, bmh_ref[...]) + _dot_nt(q, bml_ref[...])
    nl = lane % 8
    valid = nl < i
    gm = jnp.where(valid, gate, NEG)
    rank = jnp.zeros(gm.shape, F32)
    for s in range(1, 8):
        below = pltpu.roll(gm, s, axis=1)
        above = pltpu.roll(gm, LANES - s, axis=1)
        rank = rank + jnp.where((nl >= s) & (below >= gm), 1.0, 0.0)
        rank = rank + jnp.where((nl + s < 8) & (above > gm), 1.0, 0.0)
    attend = (valid & (rank < topk)) | (nl == i)
    bias = jnp.where(attend, 0.0, NEG)

    for h in range(HEADS):
        p, half = divmod(h, 2)
        qp = q[:, p * LANES:(p + 1) * LANES].astype(F32)
        if half == 0:
            q_aug = jnp.where(lane < MOBA_DH, qp, jnp.where(lane // 8 == 8 + h, bias, 0.0))
        else:
            q_aug = jnp.where(lane >= MOBA_DH, qp, jnp.where(lane // 8 == h, bias, 0.0))
        qaug_ref[h] = q_aug.astype(BF16)

    def scores_t(h, n):
        return _dot_nt(kaug_ref[h, _key_tile(n), :], qaug_ref[h])

    def values_t(h, n):
        return vtb_ref[n, h * MOBA_DH:(h + 1) * MOBA_DH, :]

    _flash_heads(i, scores_t, values_t, *flash_refs)
    _gated_heads_out(flash_refs[2], flash_refs[1], g_ref, o_ref)


def _moba_prompt(qa, kat, vat, ga, batch, seq):
    nq = seq // ATTN_TILE
    nb = seq // MOBA_BLOCK
    tile = lambda w: pl.BlockSpec((ATTN_TILE, w), lambda b, i: (b * nq + i, 0))
    whole_t = pl.BlockSpec((None, MOBA_W, seq), lambda b, i: (b, 0, 0))
    return pl.pallas_call(
        functools.partial(_moba_prompt_kernel, nb, min(MOBA_TOPK, nb)),
        out_shape=jax.ShapeDtypeStruct((batch * seq, MOBA_W), BF16),
        grid=(batch, nq),
        in_specs=[tile(MOBA_W), whole_t, whole_t, tile(MOBA_W)],
        out_specs=tile(MOBA_W),
        scratch_shapes=[pltpu.VMEM((HEADS, seq, LANES), BF16), pltpu.VMEM((nb, MOBA_W, MOBA_BLOCK), BF16),
                        pltpu.VMEM((LANES, MOBA_W), BF16), pltpu.VMEM((LANES, MOBA_W), BF16),
                        pltpu.VMEM((HEADS, ATTN_TILE, LANES), BF16)] + _flash_scratch(),
        compiler_params=pltpu.CompilerParams(dimension_semantics=("arbitrary", "arbitrary"),
                                             vmem_limit_bytes=VMEM_LIMIT),
        name="moba_prompt",
    )(qa, kat, vat, ga)


def _mla_prompt_kernel(nt, q_ref, k_ref, vt_ref, g_ref, o_ref, vtb_ref, *flash_refs):
    i = pl.program_id(1)

    @pl.when(i == 0)
    def _prepare_sequence():
        for n in range(nt):
            vtb_ref[n] = vt_ref[:, n * ATTN_TILE:(n + 1) * ATTN_TILE]

    def scores_t(h, n):
        return _dot_nt(k_ref[_key_tile(n), h * LANES:(h + 1) * LANES], q_ref[:, h * LANES:(h + 1) * LANES])

    def values_t(h, n):
        return vtb_ref[n, h * V_DIM:(h + 1) * V_DIM, :]

    _flash_heads(i, scores_t, values_t, *flash_refs)
    _gated_heads_out(flash_refs[2], flash_refs[1], g_ref, o_ref)


def _mla_prompt(qcat, kcat, vmt, gb, batch, seq):
    nq = seq // ATTN_TILE
    tile = lambda w: pl.BlockSpec((ATTN_TILE, w), lambda b, i: (b * nq + i, 0))
    whole = lambda w: pl.BlockSpec((seq, w), lambda b, i: (b, 0))
    whole_t = pl.BlockSpec((None, MLA_W, seq), lambda b, i: (b, 0, 0))
    return pl.pallas_call(
        functools.partial(_mla_prompt_kernel, nq),
        out_shape=jax.ShapeDtypeStruct((batch * seq, MLA_W), BF16),
        grid=(batch, nq),
        in_specs=[tile(HEADS * LANES), whole(HEADS * LANES), whole_t, tile(MLA_W)],
        out_specs=tile(MLA_W),
        scratch_shapes=[pltpu.VMEM((nq, MLA_W, ATTN_TILE), BF16)] + _flash_scratch(),
        compiler_params=pltpu.CompilerParams(dimension_semantics=("arbitrary", "arbitrary"),
                                             vmem_limit_bytes=VMEM_LIMIT),
        name="mla_prompt",
    )(qcat, kcat, vmt, gb)


def _merge_kernel(n_gates, x_ref, oa_ref, ob_ref, *refs):
    gate_refs = refs[:n_gates]
    g1_ref, g2_ref, wa_ref, wb_ref, wo_ref, y_ref = refs[n_gates:]
    oa, ob = oa_ref[...], ob_ref[...]
    if n_gates:
        oa = (oa * _silu(gate_refs[0][...])).astype(BF16)
        ob = (ob * _silu(gate_refs[1][...])).astype(BF16)
    a = _dot(oa, wa_ref[...])
    b = _dot(ob, wb_ref[...])
    hm = _sigmoid(g1_ref[...]) * a + _sigmoid(g2_ref[...]) * b
    y_ref[...] = x_ref[...] + _dot(hm.astype(BF16), wo_ref[...])


def _merge(x, oa, ob, branch_gates, g1, g2, wa, wb, wo):
    t = x.shape[0]
    tm = min(TOKEN_TILE, t)
    row = lambda w: pl.BlockSpec((tm, w), lambda i: (i, 0))
    return pl.pallas_call(
        functools.partial(_merge_kernel, len(branch_gates)),
        out_shape=jax.ShapeDtypeStruct((t, D_MODEL), F32),
        grid=(t // tm,),
        in_specs=[row(D_MODEL), row(MOBA_W), row(MLA_W)] + [row(g.shape[1]) for g in branch_gates] +
                 [row(D_MODEL), row(D_MODEL), _const_spec(wa.shape), _const_spec(wb.shape), _const_spec(wo.shape)],
        out_specs=row(D_MODEL),
        compiler_params=pltpu.CompilerParams(dimension_semantics=("arbitrary",), vmem_limit_bytes=VMEM_LIMIT),
        name="merge_sample" if branch_gates else "merge_prompt",
    )(x, oa, ob, *branch_gates, g1, g2, wa, wb, wo)


def _row_to_cols(row):
    eye = (lax.broadcasted_iota(jnp.int32, (MOBA_DH, MOBA_DH), 0) == lax.broadcasted_iota(jnp.int32, (MOBA_DH, MOBA_DH), 1))
    return [jnp.sum(jnp.where(eye, jnp.broadcast_to(row[:, h * MOBA_DH:(h + 1) * MOBA_DH], (MOBA_DH, MOBA_DH)), 0.0),
                    axis=1, keepdims=True) for h in range(HEADS)]


def _cols_to_row(cols):
    eye = (lax.broadcasted_iota(jnp.int32, (MOBA_DH, MOBA_DH), 0) == lax.broadcasted_iota(jnp.int32, (MOBA_DH, MOBA_DH), 1))
    return jnp.concatenate([jnp.sum(jnp.where(eye, jnp.broadcast_to(c, (MOBA_DH, MOBA_DH)), 0.0), axis=0, keepdims=True)
                            for c in cols], axis=1)


SELECT_SLOTS = 3


def _moba_select_kernel(ppc, nch, pt_ref, q_ref, kt_hbm, sel_ref, kbuf, sem, qb_ref, gate_ref):
    b = pl.program_id(0)
    c = pl.program_id(1)
    step = b * nch + c
    n_steps = pl.num_programs(0) * nch
    lane = lax.broadcasted_iota(jnp.int32, (HEADS, LANES), 1)
    low = jnp.float32(-3.0e38)

    def copies(page_of, slot):
        return [pltpu.make_async_copy(kt_hbm.at[page_of(k)], kbuf.at[slot, k], sem.at[slot]) for k in range(ppc)]

    def fetch(s):
        seq, chunk = s // nch, s % nch
        for cp in copies(lambda k: pt_ref[seq, chunk * ppc + k], s % SELECT_SLOTS):
            cp.start()

    @pl.when(step == 0)
    def _prime():
        for s in range(SELECT_SLOTS - 1):
            pl.when(s < n_steps)(functools.partial(fetch, s))

    @pl.when(step + SELECT_SLOTS - 1 < n_steps)
    def _prefetch():
        fetch(step + SELECT_SLOTS - 1)

    @pl.when(c == 0)
    def _start_sequence():
        for h, qc in enumerate(_row_to_cols(q_ref[...])):
            qb_ref[h] = jnp.broadcast_to(qc, (MOBA_DH, LANES))
        gate_ref[...] = jnp.full((HEADS, LANES), low, F32)

    slot = step % SELECT_SLOTS
    for cp in copies(lambda k: 0, slot):
        cp.wait()

    qb = qb_ref[...]
    gate = gate_ref[...]
    blocks_per_step = ppc // 2
    for j in range(blocks_per_step):
        ksum = kbuf[slot, 2 * j] + kbuf[slot, 2 * j + 1]
        part = jnp.sum(ksum * qb, axis=1)
        g = jnp.sum(part, axis=1, keepdims=True) * (1.0 / MOBA_BLOCK)
        gate = jnp.where(lane == c * blocks_per_step + j, g, gate)
    gate_ref[...] = gate

    @pl.when(c == pl.num_programs(1) - 1)
    def _pick_blocks():
        gw = gate
        out = jnp.zeros((HEADS, LANES), jnp.int32)
        for r in range(MOBA_TOPK):
            mx = jnp.max(gw, axis=1, keepdims=True)
            idx = jnp.min(jnp.where(gw == mx, lane, LANES), axis=1, keepdims=True)
            out = jnp.where(lane == r, idx, out)
            gw = jnp.where(lane == idx, low, gw)
        sel_ref[...] = out


def _moba_select(page_table, q_row, kt_pages):
    batch, n_pages = page_table.shape
    ppc = min(PAGES_PER_STEP, n_pages)
    nch = n_pages // ppc
    grid_spec = pltpu.PrefetchScalarGridSpec(
        num_scalar_prefetch=1,
        grid=(batch, nch),
        in_specs=[pl.BlockSpec((None, 1, MOBA_W), lambda b, c, pt: (b, 0, 0)), pl.BlockSpec(memory_space=pl.ANY)],
        out_specs=pl.BlockSpec((None, HEADS, LANES), lambda b, c, pt: (b, 0, 0)),
        scratch_shapes=[pltpu.VMEM((SELECT_SLOTS, ppc, HEADS, MOBA_DH, PAGE), F32),
                        pltpu.SemaphoreType.DMA((SELECT_SLOTS,)),
                        pltpu.VMEM((HEADS, MOBA_DH, LANES), F32), pltpu.VMEM((HEADS, LANES), F32)],
    )
    return pl.pallas_call(
        functools.partial(_moba_select_kernel, ppc, nch),
        out_shape=jax.ShapeDtypeStruct((batch, HEADS, LANES), jnp.int32),
        grid_spec=grid_spec,
        compiler_params=pltpu.CompilerParams(dimension_semantics=("arbitrary", "arbitrary"),
                                             vmem_limit_bytes=VMEM_LIMIT),
        name="moba_select",
    )(page_table, q_row, kt_pages)


SEL_PAGES = 2 * MOBA_TOPK


def _moba_attend_kernel(pt_ref, sel_ref, q_ref, kn_ref, vn_ref, kt_hbm, vt_hbm, o_ref, kbuf, vbuf, sem):
    b = pl.program_id(0)
    slot = b % 2

    def copies(seq, slot):
        out = []
        for h in range(HEADS):
            for r in range(MOBA_TOPK):
                blk = 0 if seq is None else sel_ref[seq, h * MOBA_TOPK + r]
                for pg in range(2):
                    page = 0 if seq is None else pt_ref[seq, 2 * blk + pg]
                    out.append(pltpu.make_async_copy(kt_hbm.at[page, h], kbuf.at[slot, h, 2 * r + pg], sem.at[0, slot]))
                    out.append(pltpu.make_async_copy(vt_hbm.at[page, h], vbuf.at[slot, h, 2 * r + pg], sem.at[1, slot]))
        return out

    @pl.when(b == 0)
    def _prime():
        for cp in copies(0, 0):
            cp.start()

    @pl.when(b + 1 < pl.num_programs(0))
    def _prefetch_next():
        for cp in copies(b + 1, 1 - slot):
            cp.start()

    for cp in copies(None, slot):
        cp.wait()

    q_cols = _row_to_cols(q_ref[...])
    k_cols = _row_to_cols(kn_ref[...])
    v_cols = _row_to_cols(vn_ref[...])
    outs = []
    for h in range(HEADS):
        qb = jnp.broadcast_to(q_cols[h], (MOBA_DH, LANES))
        s = jnp.concatenate([jnp.sum(kbuf[slot, h, j] * qb, axis=0, keepdims=True) for j in range(SEL_PAGES)], axis=0)
        sn = jnp.sum(q_cols[h] * k_cols[h], axis=0, keepdims=True)
        m = jnp.maximum(jnp.max(jnp.max(s, axis=1, keepdims=True), axis=0, keepdims=True), sn)
        p = jnp.exp(s - m)
        pn = jnp.exp(sn - m)
        l = jnp.sum(jnp.sum(p, axis=1, keepdims=True), axis=0, keepdims=True) + pn
        acc = jnp.zeros((MOBA_DH, LANES), F32)
        for j in range(SEL_PAGES):
            acc = acc + vbuf[slot, h, j] * p[j:j + 1, :]
        outs.append((jnp.sum(acc, axis=1, keepdims=True) + pn * v_cols[h]) / l)
    o_ref[...] = _cols_to_row(outs)


def _moba_attend(page_table, sel, q_row, k_row, v_row, kt_pages, vt_pages):
    batch = page_table.shape[0]
    row = lambda: pl.BlockSpec((None, 1, MOBA_W), lambda b, pt, sl: (b, 0, 0))
    grid_spec = pltpu.PrefetchScalarGridSpec(
        num_scalar_prefetch=2,
        grid=(batch,),
        in_specs=[row(), row(), row(), pl.BlockSpec(memory_space=pl.ANY), pl.BlockSpec(memory_space=pl.ANY)],
        out_specs=row(),
        scratch_shapes=[pltpu.VMEM((2, HEADS, SEL_PAGES, MOBA_DH, PAGE), F32),
                        pltpu.VMEM((2, HEADS, SEL_PAGES, MOBA_DH, PAGE), F32),
                        pltpu.SemaphoreType.DMA((2, 2))],
    )
    return pl.pallas_call(
        _moba_attend_kernel,
        out_shape=jax.ShapeDtypeStruct((batch, 1, MOBA_W), F32),
        grid_spec=grid_spec,
        compiler_params=pltpu.CompilerParams(dimension_semantics=("arbitrary",), vmem_limit_bytes=VMEM_LIMIT),
        name="moba_attend",
    )(page_table, sel, q_row, k_row, v_row, kt_pages, vt_pages)


def _mla_decode_kernel(ppc, nch, pt_ref, qn_ref, qr_ref, cnew_ref, pnew_ref, wukt_ref, wuv_ref, gkn_ref, gkr_ref,
                       cos_ref, sin_ref, cosn_ref, sinn_ref, ckv_hbm, kpe_hbm, o_ref,
                       cbuf, pbuf, sem, a_ref, m_ref, l_ref, acc_ref):
    b = pl.program_id(0)
    c = pl.program_id(1)
    step = b * nch + c
    slot = step % 2
    lhs_rows = HEADS * NOPE
    tk = ppc * PAGE

    def copies(seq, chunk, slot):
        out = []
        for k in range(ppc):
            page = 0 if seq is None else pt_ref[seq, chunk * ppc + k]
            out.append(pltpu.make_async_copy(ckv_hbm.at[page], cbuf.at[slot, k], sem.at[0, slot]))
            out.append(pltpu.make_async_copy(kpe_hbm.at[page], pbuf.at[slot, k], sem.at[1, slot]))
        return out

    @pl.when(step == 0)
    def _prime():
        for cp in copies(0, 0, 0):
            cp.start()
        a_ref[0:lhs_rows, :] = wukt_ref[...]

    last_chunk = c == nch - 1

    @pl.when(step + 1 < pl.num_programs(0) * nch)
    def _prefetch_next():
        for cp in copies(jnp.where(last_chunk, b + 1, b), jnp.where(last_chunk, 0, c + 1), 1 - slot):
            cp.start()

    @pl.when(c == 0)
    def _start_sequence():
        rowh = lax.broadcasted_iota(jnp.int32, (HEADS, MLA_W), 0)
        colh = lax.broadcasted_iota(jnp.int32, (HEADS, MLA_W), 1)
        qg = jnp.where(colh // NOPE == rowh, qn_ref[...] * gkn_ref[...], 0.0).astype(BF16)
        qlat = _dot(qg, wukt_ref[...])
        a_ref[lhs_rows:lhs_rows + 16, :] = jnp.concatenate([qlat, jnp.zeros_like(qlat)], axis=0).astype(BF16)
        m_ref[...] = jnp.full(m_ref.shape, NEG, F32)
        l_ref[...] = jnp.zeros(l_ref.shape, F32)
        acc_ref[...] = jnp.zeros(acc_ref.shape, F32)

    for cp in copies(None, None, slot):
        cp.wait()

    qr = qr_ref[...].astype(BF16)
    gkr = gkr_ref[...]

    def scores(cb, kpt, cos, sin):
        kt = _dot_nt(a_ref[...], cb)
        n_keys = cb.shape[0]
        kn = kt[0:lhs_rows, :]
        ssn = jnp.sum((kn * kn).reshape(HEADS, NOPE, n_keys), axis=1)
        sn = kt[lhs_rows:lhs_rows + HEADS, :]
        n2 = jnp.sum(kpt * kpt, axis=0, keepdims=True)
        y = kpt * gkr
        y1, y2 = y[0:ROPE // 2, :], y[ROPE // 2:ROPE, :]
        kr = jnp.concatenate([y1 * cos - y2 * sin, y2 * cos + y1 * sin], axis=0)
        sr = _dot(qr, kr.astype(BF16))
        return (sn + sr) * lax.rsqrt((ssn + n2) * (1.0 / QK_DIM) + EPS)

    def accumulate(s, cb):
        m = m_ref[...]
        m2 = jnp.maximum(m, jnp.max(s, axis=1, keepdims=True))
        a = jnp.exp(m - m2)
        p = jnp.exp(s - m2)
        l_ref[...] = a * l_ref[...] + jnp.sum(p, axis=1, keepdims=True)
        acc_ref[...] = a * acc_ref[...] + _dot(p.astype(BF16), cb)
        m_ref[...] = m2

    sub = min(MLA_SCORE_PAGES, ppc)
    cos_c, sin_c = cos_ref[c], sin_ref[c]
    cbs, ss = [], []
    for t in range(ppc // sub):
        cb = cbuf[slot, t * sub:(t + 1) * sub].reshape(sub * PAGE, KV_LORA).astype(BF16)
        kpt = jnp.concatenate([pbuf[slot, k] for k in range(t * sub, (t + 1) * sub)], axis=1)
        lo, hi = t * sub * PAGE, (t + 1) * sub * PAGE
        ss.append(scores(cb, kpt, cos_c[:, lo:hi], sin_c[:, lo:hi]))
        cbs.append(cb)
    accumulate(jnp.concatenate(ss, axis=1), jnp.concatenate(cbs, axis=0))

    @pl.when(last_chunk)
    def _finish_sequence():
        rowi = lax.broadcasted_iota(jnp.int32, (PAGE, KV_LORA), 0)
        lanei = lax.broadcasted_iota(jnp.int32, (ROPE, PAGE), 1)
        cb = jnp.where(rowi == 0, jnp.broadcast_to(cnew_ref[...], (PAGE, KV_LORA)), 0.0).astype(BF16)
        kpt = jnp.where(lanei == 0, jnp.broadcast_to(pnew_ref[...], (ROPE, PAGE)), 0.0)
        key_ok = lax.broadcasted_iota(jnp.int32, (HEADS, PAGE), 1) == 0
        accumulate(jnp.where(key_ok, scores(cb, kpt, cosn_ref[...], sinn_ref[...]), NEG), cb)
        ol = (acc_ref[...] / l_ref[...]).astype(BF16)
        res = _dot(ol, wuv_ref[...])
        rowh = lax.broadcasted_iota(jnp.int32, (HEADS, MLA_W), 0)
        colh = lax.broadcasted_iota(jnp.int32, (HEADS, MLA_W), 1)
        o_ref[...] = jnp.sum(jnp.where(colh // V_DIM == rowh, res, 0.0), axis=0, keepdims=True)


def _mla_decode(page_table, qn, qr, c_new, p_new, wukt, wuv, gkn, gkr_col, cos_t, sin_t, cos_n, sin_n,
                ckv_pages, kpet_pages):
    batch, n_pages = page_table.shape
    ppc = min(MLA_PAGES_PER_STEP, n_pages)
    nch = n_pages // ppc
    seq_spec = lambda shape: pl.BlockSpec((None,) + shape, lambda b, c, pt: (b,) + (0,) * len(shape))
    const = lambda arr: pl.BlockSpec(arr.shape, lambda b, c, pt: (0,) * arr.ndim)
    grid_spec = pltpu.PrefetchScalarGridSpec(
        num_scalar_prefetch=1,
        grid=(batch, nch),
        in_specs=[seq_spec((1, MLA_W)), seq_spec((HEADS, ROPE)), seq_spec((1, KV_LORA)), seq_spec((ROPE, 1)),
                  const(wukt), const(wuv), const(gkn), const(gkr_col), const(cos_t), const(sin_t), const(cos_n),
                  const(sin_n), pl.BlockSpec(memory_space=pl.ANY), pl.BlockSpec(memory_space=pl.ANY)],
        out_specs=seq_spec((1, MLA_W)),
        scratch_shapes=[pltpu.VMEM((2, ppc, PAGE, KV_LORA), F32), pltpu.VMEM((2, ppc, ROPE, PAGE), F32),
                        pltpu.SemaphoreType.DMA((2, 2)),
                        pltpu.VMEM((HEADS * NOPE + 16, KV_LORA), BF16), pltpu.VMEM((HEADS, 1), F32),
                        pltpu.VMEM((HEADS, 1), F32), pltpu.VMEM((HEADS, KV_LORA), F32)],
    )
    return pl.pallas_call(
        functools.partial(_mla_decode_kernel, ppc, nch),
        out_shape=jax.ShapeDtypeStruct((batch, 1, MLA_W), F32),
        grid_spec=grid_spec,
        compiler_params=pltpu.CompilerParams(dimension_semantics=("arbitrary", "arbitrary"),
                                             vmem_limit_bytes=VMEM_LIMIT),
        name="mla_decode",
    )(page_table, qn, qr, c_new, p_new, wukt, wuv, gkn, gkr_col, cos_t, sin_t, cos_n, sin_n, ckv_pages, kpet_pages)


def _rope_angles(pos, half):
    inv = np.power(np.float64(ROPE_THETA), -np.arange(half, dtype=np.float64) / half)
    return np.asarray(pos, np.float64)[:, None] * inv[None, :]


def _row_tables(pos, half):
    ang = _rope_angles(pos, half)
    k = np.arange(LANES) % (2 * half)
    cos = np.cos(ang)[:, k % half]
    sin = np.sin(ang)[:, k % half] * np.where(k < half, -1.0, 1.0).astype(np.float32)[None, :]
    return jnp.asarray(cos, F32), jnp.asarray(sin, F32)


def _block_ones(rows_per_group, groups, cols):
    m = np.zeros((groups * rows_per_group, cols), np.float32)
    m[np.arange(groups * rows_per_group), np.arange(groups * rows_per_group) // rows_per_group] = 1.0
    return m


def kernel(x_prompt, x_sample, cache_moba_k, cache_moba_v, cache_mla_ckv, cache_mla_kpe, page_table, norm_gain, w_in, moba_q_gain, moba_k_gain, mla_q_lat_gain, w_uq, mla_q_gain, mla_kv_lat_gain, w_uk, w_uv, mla_k_gain, w_branch_a, w_branch_b, w_out):
    batch, seq, _ = x_prompt.shape
    dec_batch, dec_seq, _ = x_sample.shape
    depth = norm_gain.shape[0]
    assert depth == 1 and dec_seq == 1
    n_pages = page_table.shape[1]
    past = n_pages * PAGE
    assert seq % ATTN_TILE == 0 and seq // MOBA_BLOCK <= 8 and past % MOBA_BLOCK == 0
    assert past // MOBA_BLOCK >= MOBA_TOPK
    assert n_pages % min(PAGES_PER_STEP, n_pages) == 0 and n_pages % min(MLA_PAGES_PER_STEP, n_pages) == 0

    w = w_in[0]
    bf = lambda a: a.astype(BF16)
    o = 4 * MOBA_W
    wqk, wv, wg = bf(w[:, 0:2 * MOBA_W]), bf(w[:, 2 * MOBA_W:3 * MOBA_W]), bf(w[:, 3 * MOBA_W:o])
    w2 = bf(jnp.pad(w[:, o:o + Q_LORA + KV_LORA + ROPE], ((0, 0), (0, LANES - ROPE))))
    o += Q_LORA + KV_LORA + ROPE
    w3 = bf(w[:, o:o + MLA_W + 2 * D_MODEL])
    wuqn = bf(w_uq[0][:, :, :NOPE].reshape(Q_LORA, HEADS * NOPE))
    wuqr = bf(w_uq[0][:, :, NOPE:].reshape(Q_LORA, HEADS * ROPE))
    wuk = bf(w_uk[0].reshape(KV_LORA, HEADS * NOPE))
    wuv = bf(w_uv[0].reshape(KV_LORA, MLA_W))
    tile8 = lambda g: jnp.tile(g, HEADS)[None, :].astype(F32)
    gains = [tile8(moba_q_gain[0]), tile8(moba_k_gain[0]), mla_q_lat_gain[0][None, :], mla_kv_lat_gain[0][None, :],
             tile8(mla_q_gain[0][:NOPE]), tile8(mla_q_gain[0][NOPE:]), tile8(mla_k_gain[0][:NOPE]),
             tile8(mla_k_gain[0][NOPE:])]
    place = lambda width, off: np.eye(HEADS * LANES, dtype=np.float32)[
        (np.arange(HEADS * width) // width) * LANES + off + np.arange(HEADS * width) % width]
    helpers = [
        jnp.asarray(_block_ones(MOBA_DH, HEADS, HEADS) @ _block_ones(MOBA_DH, HEADS, HEADS).T, BF16),
        jnp.asarray(_block_ones(NOPE, HEADS, LANES), BF16),
        jnp.asarray(_block_ones(ROPE, HEADS, LANES), BF16),
        jnp.asarray(_block_ones(NOPE, HEADS, LANES).T, BF16),
        jnp.asarray(_block_ones(ROPE, HEADS, LANES).T, BF16),
        jnp.asarray(np.concatenate([np.tile(np.eye(ROPE, dtype=np.float32), (1, HEADS)),
                                    np.zeros((LANES - ROPE, HEADS * ROPE), np.float32)], axis=0), BF16),
        jnp.asarray(place(NOPE, 0), BF16),
        jnp.asarray(place(ROPE, NOPE), BF16),
    ]
    consts_p = [norm_gain[0][None, :], wqk, wv.T, wg, w2, w3, wuqn, wuqr, wuk, wuv.T] + gains + helpers
    consts_s = [norm_gain[0][None, :], wqk, wv, wg, w2, w3, wuqn, wuqr, wuk, wuv] + gains + helpers
    assert len(consts_p) - len(helpers) == N_FRONT_WEIGHTS
    pos_p = np.arange(seq)
    pos_s = np.full((dec_batch,), past)
    tables_p = _row_tables(pos_p, MOBA_DH // 2) + _row_tables(pos_p, ROPE // 2)
    tables_s = _row_tables(pos_s, MOBA_DH // 2) + _row_tables(pos_s, ROPE // 2)
    wa = bf(w_branch_a[0])
    wb = bf(w_branch_b[0])
    wo = bf(w_out[0])

    xp = x_prompt.reshape(batch * seq, D_MODEL)
    (qa, kat, vat, ga, ckv, kpe, gb, g1, g2, qcat, kcat, vmt) = _front(xp, consts_p, tables_p, True, seq)
    oa = _moba_prompt(qa, kat, vat, ga, batch, seq)
    ob = _mla_prompt(qcat, kcat, vmt, gb, batch, seq)
    y_prompt = _merge(xp, oa, ob, (), g1, g2, wa, wb, wo).reshape(batch, seq, D_MODEL)

    xs = x_sample.reshape(dec_batch, D_MODEL)
    (qa_s, ka_s, va_s, ga_s, ckv_s, kpe_s, gb_s, g1_s, g2_s, qn_s, qr_s) = _front(xs, consts_s, tables_s, False, 1)
    kt_pages = jnp.transpose(cache_moba_k[0], (0, 2, 3, 1))
    vt_pages = jnp.transpose(cache_moba_v[0], (0, 2, 3, 1))
    ckv_pages = cache_mla_ckv[0]
    kpet_pages = jnp.transpose(cache_mla_kpe[0], (0, 2, 1))
    row3 = lambda a: a.reshape(dec_batch, 1, MOBA_W)
    sel = _moba_select(page_table, row3(qa_s), kt_pages)[:, :, :MOBA_TOPK].reshape(dec_batch, HEADS * MOBA_TOPK)
    oa_s = _moba_attend(page_table, sel, row3(qa_s), row3(ka_s), row3(va_s), kt_pages, vt_pages).reshape(dec_batch, MOBA_W)

    ppc = min(MLA_PAGES_PER_STEP, n_pages)
    ang_past = _rope_angles(np.arange(past), ROPE // 2).T
    ang_past = ang_past.reshape(ROPE // 2, n_pages // ppc, ppc * PAGE).transpose(1, 0, 2)
    ang_new = np.broadcast_to(_rope_angles(np.array([past]), ROPE // 2).T, (ROPE // 2, PAGE))
    wukt = jnp.transpose(w_uk[0].reshape(KV_LORA, HEADS * NOPE)).astype(BF16)
    ob_s = _mla_decode(
        page_table, qn_s.reshape(dec_batch, 1, MLA_W), qr_s.reshape(dec_batch, HEADS, ROPE),
        ckv_s.reshape(dec_batch, 1, KV_LORA), kpe_s.reshape(dec_batch, ROPE, 1),
        wukt, wuv, tile8(mla_k_gain[0][:NOPE]), mla_k_gain[0][NOPE:].reshape(ROPE, 1),
        jnp.asarray(np.cos(ang_past), F32), jnp.asarray(np.sin(ang_past), F32),
        jnp.asarray(np.cos(ang_new), F32), jnp.asarray(np.sin(ang_new), F32),
        ckv_pages, kpet_pages).reshape(dec_batch, MLA_W)
    y_sample = _merge(xs, oa_s, ob_s, (ga_s, gb_s), g1_s, g2_s, wa, wb, wo).reshape(dec_batch, 1, D_MODEL)

    p5 = lambda a_t: a_t.reshape(1, batch, HEADS, MOBA_DH, seq).transpose(0, 1, 4, 2, 3)
    s5 = lambda a, w: a.reshape(1, dec_batch, 1, HEADS, w)
    return (y_prompt, y_sample,
            p5(kat), p5(vat), ckv.reshape(1, batch, seq, KV_LORA), kpe.reshape(1, batch, seq, ROPE),
            s5(ka_s, MOBA_DH), s5(va_s, MOBA_DH), ckv_s.reshape(1, dec_batch, 1, KV_LORA),
            kpe_s.reshape(1, dec_batch, 1, ROPE))
```

```python
import functools

import jax
import jax.numpy as jnp
import numpy as np
from jax import lax
from jax.experimental import pallas as pl
from jax.experimental.pallas import tpu as pltpu

F32 = jnp.float32
BF16 = jnp.bfloat16

D_MODEL = 1024
HEADS = 8
MOBA_DH = 64
MOBA_W = HEADS * MOBA_DH
MOBA_BLOCK = 256
MOBA_TOPK = 3
NOPE = 64
ROPE = 32
QK_DIM = NOPE + ROPE
V_DIM = 64
MLA_W = HEADS * V_DIM
Q_LORA = 384
KV_LORA = 256
PAGE = 128
ROPE_THETA = 10000.0
EPS = 1e-6
NEG = -1e30

LANES = 128
V7X_VMEM_BYTES = 64 * 1024 * 1024
VMEM_LIMIT = V7X_VMEM_BYTES - 8 * 1024 * 1024

TOKEN_TILE = 256
ATTN_TILE = MOBA_BLOCK
PAGES_PER_STEP = 16
MLA_PAGES_PER_STEP = 32
MLA_SCORE_PAGES = 16


def _dot(a, b):
    return jnp.dot(a, b, preferred_element_type=F32)


def _dot_nt(a, b):
    return lax.dot_general(a, b, (((1,), (1,)), ((), ())), preferred_element_type=F32)


def _dot_split(x, w):
    hi = x.astype(BF16)
    lo = (x - hi.astype(F32)).astype(BF16)
    return _dot(hi, w) + _dot(lo, w)


def _sigmoid(x):
    return 1.0 / (1.0 + jnp.exp(-x))


def _silu(x):
    return x * _sigmoid(x)


def _rope_rows(y, cos, sin_signed, half):
    lane = lax.broadcasted_iota(jnp.int32, (y.shape[0], LANES), 1)
    first = (lane % (2 * half)) < half
    outs = []
    for c in range(y.shape[1] // LANES):
        yc = y[:, c * LANES:(c + 1) * LANES]
        up = pltpu.roll(yc, LANES - half, axis=1)
        dn = pltpu.roll(yc, half, axis=1)
        outs.append(yc * cos + jnp.where(first, up, dn) * sin_signed)
    return jnp.concatenate(outs, axis=1)


def _front_kernel(prompt, x_ref, ng_ref, wqk_ref, wv_ref, wg_ref, w2_ref, w3_ref, wuqn_ref, wuqr_ref, wuk_ref, wuv_ref,
                  gq_ref, gk_ref, gcq_ref, gckv_ref, gqn_ref, gqr_ref, gkn_ref, gkr_ref,
                  cos64_ref, sin64_ref, cos32_ref, sin32_ref,
                  g64_ref, s64_ref, s32_ref, e64_ref, e32_ref, t32_ref, pn_ref, pr_ref,
                  qa_ref, ka_ref, v_ref, ga_ref, ckv_ref, kpe_ref, gb_ref, g1_ref, g2_ref, *mode_refs):
    x = x_ref[...]
    h = (x * lax.rsqrt(jnp.mean(x * x, axis=-1, keepdims=True) + EPS) * ng_ref[...]).astype(BF16)

    z1 = _dot(h, wqk_ref[...])
    g64 = g64_ref[...]
    cos64, sin64 = cos64_ref[...], sin64_ref[...]

    def head_norm64(z, gain):
        ss = _dot_split(z * z, g64)
        return z * lax.rsqrt(ss * (1.0 / MOBA_DH) + EPS) * gain

    qa = _rope_rows(head_norm64(z1[:, 0:MOBA_W], gq_ref[...]), cos64, sin64, MOBA_DH // 2)
    qa_ref[...] = (qa * MOBA_DH ** -0.5).astype(qa_ref.dtype)
    ka = _rope_rows(head_norm64(z1[:, MOBA_W:2 * MOBA_W], gk_ref[...]), cos64, sin64, MOBA_DH // 2)
    if prompt:
        ka_ref[...] = jnp.transpose(ka)
        v_ref[...] = _dot_nt(wv_ref[...], h)
    else:
        ka_ref[...] = ka
        v_ref[...] = _dot(h, wv_ref[...])
    ga_ref[...] = _dot(h, wg_ref[...])

    z2 = _dot(h, w2_ref[...])
    cq = z2[:, 0:Q_LORA]
    cqn = (cq * lax.rsqrt(jnp.mean(cq * cq, axis=-1, keepdims=True) + EPS) * gcq_ref[...]).astype(BF16)
    qn = _dot(cqn, wuqn_ref[...])
    qr = _dot(cqn, wuqr_ref[...])
    s64, s32, e64, e32 = s64_ref[...], s32_ref[...], e64_ref[...], e32_ref[...]
    ss = _dot_split(qn * qn, s64) + _dot_split(qr * qr, s32)
    r = lax.rsqrt(ss * (1.0 / QK_DIM) + EPS)
    scale = QK_DIM ** -0.5
    cos32, sin32 = cos32_ref[...], sin32_ref[...]
    qn_n = qn * _dot_split(r, e64) * (gqn_ref[...] * scale)
    qr_n = _rope_rows(qr * _dot_split(r, e32) * (gqr_ref[...] * scale), cos32, sin32, ROPE // 2)

    ckv_raw = z2[:, Q_LORA:Q_LORA + KV_LORA]
    ckv = ckv_raw * lax.rsqrt(jnp.mean(ckv_raw * ckv_raw, axis=-1, keepdims=True) + EPS) * gckv_ref[...]
    ckv_ref[...] = ckv
    kpe_p = z2[:, Q_LORA + KV_LORA:Q_LORA + KV_LORA + LANES]
    kpe_ref[...] = kpe_p[:, 0:ROPE]

    z3 = _dot(h, w3_ref[...])
    gb_ref[...] = z3[:, 0:MLA_W]
    g1_ref[...] = z3[:, MLA_W:MLA_W + D_MODEL]
    g2_ref[...] = z3[:, MLA_W + D_MODEL:MLA_W + 2 * D_MODEL]

    if not prompt:
        qn_ref, qr_ref = mode_refs
        qn_ref[...] = qn_n
        qr_ref[...] = qr_n
        return
    qcat_ref, kcat_ref, vmt_ref = mode_refs
    pn, pr = pn_ref[...], pr_ref[...]
    qcat_ref[...] = (_dot(qn_n.astype(BF16), pn) + _dot(qr_n.astype(BF16), pr)).astype(BF16)
    cb = ckv.astype(BF16)
    kn = _dot(cb, wuk_ref[...])
    ssk = _dot_split(kn * kn, s64) + jnp.sum(kpe_p * kpe_p, axis=-1, keepdims=True)
    rk = lax.rsqrt(ssk * (1.0 / QK_DIM) + EPS)
    kn_n = (kn * _dot_split(rk, e64) * gkn_ref[...]).astype(BF16)
    kt = _dot_split(kpe_p, t32_ref[...]) * gkr_ref[...]
    kr_n = (_rope_rows(kt, cos32, sin32, ROPE // 2) * _dot_split(rk, e32)).astype(BF16)
    kcat_ref[...] = (_dot(kn_n, pn) + _dot(kr_n, pr)).astype(BF16)
    vmt_ref[...] = _dot_nt(wuv_ref[...], cb).astype(BF16)


def _const_spec(shape):
    nd = len(shape)
    return pl.BlockSpec(shape, lambda *_: (0,) * nd, pipeline_mode=pl.Buffered(1))


N_FRONT_WEIGHTS = 18


def _front(x, consts, tables, prompt, seq):
    t = x.shape[0]
    tm = min(TOKEN_TILE, t)
    n_tab = tables[0].shape[0] // tm
    row = lambda w: pl.BlockSpec((tm, w), lambda i: (i, 0))
    tab = lambda: pl.BlockSpec((tm, LANES), lambda i: (i % n_tab, 0))
    in_specs = [row(D_MODEL)] + [_const_spec(c.shape) for c in consts[:N_FRONT_WEIGHTS]] + [tab() for _ in range(4)] + \
               [_const_spec(c.shape) for c in consts[N_FRONT_WEIGHTS:]]
    q_dtype = BF16 if prompt else F32
    rows = lambda w, d: (jax.ShapeDtypeStruct((t, w), d), row(w))
    if prompt:
        nq = seq // tm
        cols = lambda d: (jax.ShapeDtypeStruct((t // seq, MOBA_W, seq), d),
                          pl.BlockSpec((None, MOBA_W, tm), lambda i: (i // nq, 0, i % nq)))
        kv_out = cols(F32)
        mode_outs = [rows(HEADS * LANES, BF16), rows(HEADS * LANES, BF16), cols(BF16)]
    else:
        kv_out = rows(MOBA_W, F32)
        mode_outs = [rows(MLA_W, F32), rows(HEADS * ROPE, F32)]
    outs = [rows(MOBA_W, q_dtype), kv_out, kv_out, rows(MOBA_W, F32), rows(KV_LORA, F32), rows(ROPE, F32),
            rows(MLA_W, F32), rows(D_MODEL, F32), rows(D_MODEL, F32)] + mode_outs
    return pl.pallas_call(
        functools.partial(_front_kernel, prompt),
        out_shape=[o[0] for o in outs],
        grid=(t // tm,),
        in_specs=in_specs,
        out_specs=[o[1] for o in outs],
        compiler_params=pltpu.CompilerParams(dimension_semantics=("arbitrary",), vmem_limit_bytes=VMEM_LIMIT),
        name="front_prompt" if prompt else "front_sample",
    )(x, *consts[:N_FRONT_WEIGHTS], *tables, *consts[N_FRONT_WEIGHTS:])


def _flash_heads(i, scores_t, values_t, side_work, m_ref, l_ref, acc_ref, s_ref, p_ref):
    m_ref[...] = jnp.full(m_ref.shape, NEG, F32)
    l_ref[...] = jnp.zeros(l_ref.shape, F32)
    acc_ref[...] = jnp.zeros(acc_ref.shape, F32)
    key = lax.broadcasted_iota(jnp.int32, (ATTN_TILE, ATTN_TILE), 0)
    qry = lax.broadcasted_iota(jnp.int32, (ATTN_TILE, ATTN_TILE), 1)
    causal = key <= qry

    def body(n, carry):
        side_work(n)
        keep = jnp.logical_or(causal, n < i)
        for h in range(HEADS):
            s_ref[h] = scores_t(h, n)
        for h in range(HEADS):
            s = jnp.where(keep, s_ref[h], NEG)
            m = m_ref[h]
            m2 = jnp.maximum(m, jnp.max(s, axis=0, keepdims=True))
            a = jnp.exp(m - m2)
            p = jnp.exp(s - m2)
            l_ref[h] = a * l_ref[h] + jnp.sum(p, axis=0, keepdims=True)
            acc_ref[h] = a * acc_ref[h]
            p_ref[h] = p.astype(BF16)
            m_ref[h] = m2
        for h in range(HEADS):
            acc_ref[h] = acc_ref[h] + _dot(values_t(h, n), p_ref[h])
        return carry

    lax.fori_loop(0, i + 1, body, 0)


def _gated_heads_out(acc_ref, l_ref, g_ref, o_ref):
    o_t = jnp.concatenate([acc_ref[h] / l_ref[h] for h in range(HEADS)], axis=0)
    o_ref[...] = (jnp.transpose(o_t) * _silu(g_ref[...])).astype(o_ref.dtype)


def _flash_scratch():
    return [pltpu.VMEM((HEADS, 1, ATTN_TILE), F32), pltpu.VMEM((HEADS, 1, ATTN_TILE), F32),
            pltpu.VMEM((HEADS, V_DIM, ATTN_TILE), F32), pltpu.VMEM((HEADS, ATTN_TILE, ATTN_TILE), F32),
            pltpu.VMEM((HEADS, ATTN_TILE, ATTN_TILE), BF16)]


def _key_tile(n):
    return pl.ds(pl.multiple_of(n * ATTN_TILE, ATTN_TILE), ATTN_TILE)


def _moba_prompt_kernel(nb, topk, select, pt_ref, q_ref, kt_ref, vt_ref, g_ref, qs_ref, kt_hbm, o_ref, sel_ref,
                        kaug_ref, vtb_ref, bmh_ref, bml_ref, qaug_ref, *refs):
    i = pl.program_id(1)
    flash_refs, select_refs = refs[:5], refs[5:]
    lane = lax.broadcasted_iota(jnp.int32, (ATTN_TILE, LANES), 1)

    @pl.when(i == 0)
    def _prepare_sequence():
        for n in range(nb):
            vtb_ref[n] = vt_ref[:, n * MOBA_BLOCK:(n + 1) * MOBA_BLOCK].astype(BF16)
        k_blocks = [jnp.transpose(kt_ref[:, n * MOBA_BLOCK:(n + 1) * MOBA_BLOCK]) for n in range(nb)]
        means = [jnp.mean(k_blocks[n], axis=0, keepdims=True) if n < nb
                 else jnp.zeros((1, MOBA_W), F32) for n in range(8)]
        km = jnp.concatenate(means, axis=0)
        col = lax.broadcasted_iota(jnp.int32, (8, MOBA_W), 1)
        per_head = [jnp.where(col // MOBA_DH == h, km, 0.0) for h in range(HEADS)]
        bm = jnp.concatenate(per_head + per_head, axis=0)
        hi = bm.astype(BF16)
        bmh_ref[...] = hi
        bml_ref[...] = (bm - hi.astype(F32)).astype(BF16)
        for n in range(nb):
            for h in range(HEADS):
                p, half = divmod(h, 2)
                kp = k_blocks[n][:, p * LANES:(p + 1) * LANES]
                keep = (lane < MOBA_DH) if half == 0 else (lane >= MOBA_DH)
                hot = (MOBA_DH + 8 * h + n) if half == 0 else (8 * h + n)
                kaug_ref[h, n * MOBA_BLOCK:(n + 1) * MOBA_BLOCK, :] = jnp.where(
                    keep, kp, jnp.where(lane == hot, 1.0, 0.0)).astype(BF16)

    q = q_ref[...]
    gate = _dot_nt(q, bmh_ref[...]) + _dot_nt(q, bml_ref[...])
    nl = lane % 8
    valid = nl < i
    gm = jnp.where(valid, gate, NEG)
    rank = jnp.zeros(gm.shape, F32)
    for s in range(1, 8):
        below = pltpu.roll(gm, s, axis=1)
        above = pltpu.roll(gm, LANES - s, axis=1)
        rank = rank + jnp.where((nl >= s) & (below >= gm), 1.0, 0.0)
        rank = rank + jnp.where((nl + s < 8) & (above > gm), 1.0, 0.0)
    attend = (valid & (rank < topk)) | (nl == i)
    bias = jnp.where(attend, 0.0, NEG)

    for h in range(HEADS):
        p, half = divmod(h, 2)
        qp = q[:, p * LANES:(p + 1) * LANES].astype(F32)
        if half == 0:
            q_aug = jnp.where(lane < MOBA_DH, qp, jnp.where(lane // 8 == 8 + h, bias, 0.0))
        else:
            q_aug = jnp.where(lane >= MOBA_DH, qp, jnp.where(lane // 8 == h, bias, 0.0))
        qaug_ref[h] = q_aug.astype(BF16)

    def scores_t(h, n):
        return _dot_nt(kaug_ref[h, _key_tile(n), :], qaug_ref[h])

    def values_t(h, n):
        return vtb_ref[n, h * MOBA_DH:(h + 1) * MOBA_DH, :]

    def side_work(n):
        _select_step(_loop_iteration(nb, n), *select, pt_ref, qs_ref, kt_hbm, sel_ref, *select_refs)

    _flash_heads(i, scores_t, values_t, side_work, *flash_refs)
    _gated_heads_out(flash_refs[2], flash_refs[1], g_ref, o_ref)


def _loop_iteration(nq, n):
    b, i = pl.program_id(0), pl.program_id(1)
    return b * (nq * (nq + 1) // 2) + (i * (i + 1)) // 2 + n


def _select_operands(page_table, qs, kt_pages, select):
    _, _, ppc, _ = select
    dec_batch = qs.shape[0]
    in_specs = [pl.BlockSpec(qs.shape, lambda b, i, pt: (0, 0)), pl.BlockSpec(memory_space=pl.ANY)]
    out_shape = jax.ShapeDtypeStruct((dec_batch + 1, HEADS, LANES), jnp.int32)
    out_spec = pl.BlockSpec((dec_batch + 1, HEADS, LANES), lambda b, i, pt: (0, 0, 0))
    return in_specs, out_shape, out_spec, _select_scratch(ppc)


def _moba_prompt(qa, kat, vat, ga, batch, seq, page_table, qs, kt_pages, select):
    nq = seq // ATTN_TILE
    nb = seq // MOBA_BLOCK
    tile = lambda w: pl.BlockSpec((ATTN_TILE, w), lambda b, i, pt: (b * nq + i, 0))
    whole_t = pl.BlockSpec((None, MOBA_W, seq), lambda b, i, pt: (b, 0, 0))
    sel_in, sel_shape, sel_spec, sel_scratch = _select_operands(page_table, qs, kt_pages, select)
    grid_spec = pltpu.PrefetchScalarGridSpec(
        num_scalar_prefetch=1,
        grid=(batch, nq),
        in_specs=[tile(MOBA_W), whole_t, whole_t, tile(MOBA_W)] + sel_in,
        out_specs=[tile(MOBA_W), sel_spec],
        scratch_shapes=[pltpu.VMEM((HEADS, seq, LANES), BF16), pltpu.VMEM((nb, MOBA_W, MOBA_BLOCK), BF16),
                        pltpu.VMEM((LANES, MOBA_W), BF16), pltpu.VMEM((LANES, MOBA_W), BF16),
                        pltpu.VMEM((HEADS, ATTN_TILE, LANES), BF16)] + _flash_scratch() + sel_scratch,
    )
    return pl.pallas_call(
        functools.partial(_moba_prompt_kernel, nb, min(MOBA_TOPK, nb), select),
        out_shape=[jax.ShapeDtypeStruct((batch * seq, MOBA_W), BF16), sel_shape],
        grid_spec=grid_spec,
        compiler_params=pltpu.CompilerParams(dimension_semantics=("arbitrary", "arbitrary"),
                                             vmem_limit_bytes=VMEM_LIMIT),
        name="moba_prompt",
    )(page_table, qa, kat, vat, ga, qs, kt_pages)


def _mla_prompt_kernel(nt, select, pt_ref, q_ref, k_ref, vt_ref, g_ref, qs_ref, kt_hbm, o_ref, sel_ref, vtb_ref, *refs):
    i = pl.program_id(1)
    flash_refs, select_refs = refs[:5], refs[5:]

    @pl.when(i == 0)
    def _prepare_sequence():
        for n in range(nt):
            vtb_ref[n] = vt_ref[:, n * ATTN_TILE:(n + 1) * ATTN_TILE]

    def scores_t(h, n):
        return _dot_nt(k_ref[_key_tile(n), h * LANES:(h + 1) * LANES], q_ref[:, h * LANES:(h + 1) * LANES])

    def values_t(h, n):
        return vtb_ref[n, h * V_DIM:(h + 1) * V_DIM, :]

    def side_work(n):
        _select_step(_loop_iteration(nt, n), *select, pt_ref, qs_ref, kt_hbm, sel_ref, *select_refs)

    _flash_heads(i, scores_t, values_t, side_work, *flash_refs)
    _gated_heads_out(flash_refs[2], flash_refs[1], g_ref, o_ref)


def _mla_prompt(qcat, kcat, vmt, gb, batch, seq, page_table, qs, kt_pages, select):
    nq = seq // ATTN_TILE
    tile = lambda w: pl.BlockSpec((ATTN_TILE, w), lambda b, i, pt: (b * nq + i, 0))
    whole = lambda w: pl.BlockSpec((seq, w), lambda b, i, pt: (b, 0))
    whole_t = pl.BlockSpec((None, MLA_W, seq), lambda b, i, pt: (b, 0, 0))
    sel_in, sel_shape, sel_spec, sel_scratch = _select_operands(page_table, qs, kt_pages, select)
    grid_spec = pltpu.PrefetchScalarGridSpec(
        num_scalar_prefetch=1,
        grid=(batch, nq),
        in_specs=[tile(HEADS * LANES), whole(HEADS * LANES), whole_t, tile(MLA_W)] + sel_in,
        out_specs=[tile(MLA_W), sel_spec],
        scratch_shapes=[pltpu.VMEM((nq, MLA_W, ATTN_TILE), BF16)] + _flash_scratch() + sel_scratch,
    )
    return pl.pallas_call(
        functools.partial(_mla_prompt_kernel, nq, select),
        out_shape=[jax.ShapeDtypeStruct((batch * seq, MLA_W), BF16), sel_shape],
        grid_spec=grid_spec,
        compiler_params=pltpu.CompilerParams(dimension_semantics=("arbitrary", "arbitrary"),
                                             vmem_limit_bytes=VMEM_LIMIT),
        name="mla_prompt",
    )(page_table, qcat, kcat, vmt, gb, qs, kt_pages)


def _merge_kernel(n_gates, x_ref, oa_ref, ob_ref, *refs):
    gate_refs = refs[:n_gates]
    g1_ref, g2_ref, wa_ref, wb_ref, wo_ref, y_ref = refs[n_gates:]
    oa, ob = oa_ref[...], ob_ref[...]
    if n_gates:
        oa = (oa * _silu(gate_refs[0][...])).astype(BF16)
        ob = (ob * _silu(gate_refs[1][...])).astype(BF16)
    a = _dot(oa, wa_ref[...])
    b = _dot(ob, wb_ref[...])
    hm = _sigmoid(g1_ref[...]) * a + _sigmoid(g2_ref[...]) * b
    y_ref[...] = x_ref[...] + _dot(hm.astype(BF16), wo_ref[...])


def _merge(x, oa, ob, branch_gates, g1, g2, wa, wb, wo):
    t = x.shape[0]
    tm = min(TOKEN_TILE, t)
    row = lambda w: pl.BlockSpec((tm, w), lambda i: (i, 0))
    return pl.pallas_call(
        functools.partial(_merge_kernel, len(branch_gates)),
        out_shape=jax.ShapeDtypeStruct((t, D_MODEL), F32),
        grid=(t // tm,),
        in_specs=[row(D_MODEL), row(MOBA_W), row(MLA_W)] + [row(g.shape[1]) for g in branch_gates] +
                 [row(D_MODEL), row(D_MODEL), _const_spec(wa.shape), _const_spec(wb.shape), _const_spec(wo.shape)],
        out_specs=row(D_MODEL),
        compiler_params=pltpu.CompilerParams(dimension_semantics=("arbitrary",), vmem_limit_bytes=VMEM_LIMIT),
        name="merge_sample" if branch_gates else "merge_prompt",
    )(x, oa, ob, *branch_gates, g1, g2, wa, wb, wo)


def _row_to_cols(row):
    eye = (lax.broadcasted_iota(jnp.int32, (MOBA_DH, MOBA_DH), 0) == lax.broadcasted_iota(jnp.int32, (MOBA_DH, MOBA_DH), 1))
    return [jnp.sum(jnp.where(eye, jnp.broadcast_to(row[:, h * MOBA_DH:(h + 1) * MOBA_DH], (MOBA_DH, MOBA_DH)), 0.0),
                    axis=1, keepdims=True) for h in range(HEADS)]


def _cols_to_row(cols):
    eye = (lax.broadcasted_iota(jnp.int32, (MOBA_DH, MOBA_DH), 0) == lax.broadcasted_iota(jnp.int32, (MOBA_DH, MOBA_DH), 1))
    return jnp.concatenate([jnp.sum(jnp.where(eye, jnp.broadcast_to(c, (MOBA_DH, MOBA_DH)), 0.0), axis=0, keepdims=True)
                            for c in cols], axis=1)


SELECT_SLOTS = 3


def _select_scratch(ppc):
    return [pltpu.VMEM((SELECT_SLOTS, ppc, HEADS, MOBA_DH, PAGE), F32), pltpu.SemaphoreType.DMA((SELECT_SLOTS,)),
            pltpu.VMEM((HEADS, LANES), F32)]


def _select_step(it, first_chunk, n_chunks, ppc, nch, pt_ref, q_ref, kt_hbm, sel_ref, kbuf, sem, gate_ref):
    lane = lax.broadcasted_iota(jnp.int32, (HEADS, LANES), 1)
    low = jnp.float32(-3.0e38)

    def copies(page_of, slot):
        return [pltpu.make_async_copy(kt_hbm.at[page_of(k)], kbuf.at[slot, k], sem.at[slot]) for k in range(ppc)]

    def fetch(j):
        g = first_chunk + j
        seq, chunk = g // nch, g % nch
        for cp in copies(lambda k: pt_ref[seq, chunk * ppc + k], j % SELECT_SLOTS):
            cp.start()

    @pl.when(it == 0)
    def _prime():
        sel_ref[...] = jnp.zeros(sel_ref.shape, jnp.int32)
        gate_ref[...] = jnp.full((HEADS, LANES), low, F32)
        for j in range(n_chunks, SELECT_SLOTS):
            kbuf[j] = jnp.zeros(kbuf.shape[1:], F32)
        for j in range(min(SELECT_SLOTS - 1, n_chunks)):
            fetch(j)

    @pl.when(it + SELECT_SLOTS - 1 < n_chunks)
    def _prefetch():
        fetch(it + SELECT_SLOTS - 1)

    if n_chunks == 0:
        return
    active = it < n_chunks
    slot = it % SELECT_SLOTS

    @pl.when(active)
    def _wait():
        for cp in copies(lambda k: 0, slot):
            cp.wait()

    g = first_chunk + jnp.minimum(it, n_chunks - 1)
    seq, c = g // nch, g % nch
    qb = jnp.stack([jnp.broadcast_to(qc, (MOBA_DH, LANES)) for qc in _row_to_cols(q_ref[pl.ds(seq, 1), :])], axis=0)
    gate = jnp.where(c == 0, low, gate_ref[...])
    blocks_per_step = ppc // 2
    for j in range(blocks_per_step):
        ksum = kbuf[slot, 2 * j] + kbuf[slot, 2 * j + 1]
        part = jnp.sum(ksum * qb, axis=1)
        gj = jnp.sum(part, axis=1, keepdims=True) * (1.0 / MOBA_BLOCK)
        gate = jnp.where(lane == c * blocks_per_step + j, gj, gate)
    gate_ref[...] = gate

    gw = gate
    out = jnp.zeros((HEADS, LANES), jnp.int32)
    for r in range(MOBA_TOPK):
        mx = jnp.max(gw, axis=1, keepdims=True)
        idx = jnp.min(jnp.where(gw == mx, lane, LANES), axis=1, keepdims=True)
        out = jnp.where(lane == r, idx, out)
        gw = jnp.where(lane == idx, low, gw)
    spare_row = sel_ref.shape[0] - 1
    sel_ref[jnp.where(jnp.logical_and(active, c == nch - 1), seq, spare_row)] = out


SEL_PAGES = 2 * MOBA_TOPK


def _moba_attend_kernel(pt_ref, sel_ref, q_ref, kn_ref, vn_ref, kt_hbm, vt_hbm, o_ref, kbuf, vbuf, sem):
    b = pl.program_id(0)
    slot = b % 2

    def copies(seq, slot):
        out = []
        for h in range(HEADS):
            for r in range(MOBA_TOPK):
                blk = 0 if seq is None else sel_ref[seq, h * MOBA_TOPK + r]
                for pg in range(2):
                    page = 0 if seq is None else pt_ref[seq, 2 * blk + pg]
                    out.append(pltpu.make_async_copy(kt_hbm.at[page, h], kbuf.at[slot, h, 2 * r + pg], sem.at[0, slot]))
                    out.append(pltpu.make_async_copy(vt_hbm.at[page, h], vbuf.at[slot, h, 2 * r + pg], sem.at[1, slot]))
        return out

    @pl.when(b == 0)
    def _prime():
        for cp in copies(0, 0):
            cp.start()

    @pl.when(b + 1 < pl.num_programs(0))
    def _prefetch_next():
        for cp in copies(b + 1, 1 - slot):
            cp.start()

    for cp in copies(None, slot):
        cp.wait()

    q_cols = _row_to_cols(q_ref[...])
    k_cols = _row_to_cols(kn_ref[...])
    v_cols = _row_to_cols(vn_ref[...])
    outs = []
    for h in range(HEADS):
        qb = jnp.broadcast_to(q_cols[h], (MOBA_DH, LANES))
        s = jnp.concatenate([jnp.sum(kbuf[slot, h, j] * qb, axis=0, keepdims=True) for j in range(SEL_PAGES)], axis=0)
        sn = jnp.sum(q_cols[h] * k_cols[h], axis=0, keepdims=True)
        m = jnp.maximum(jnp.max(jnp.max(s, axis=1, keepdims=True), axis=0, keepdims=True), sn)
        p = jnp.exp(s - m)
        pn = jnp.exp(sn - m)
        l = jnp.sum(jnp.sum(p, axis=1, keepdims=True), axis=0, keepdims=True) + pn
        acc = jnp.zeros((MOBA_DH, LANES), F32)
        for j in range(SEL_PAGES):
            acc = acc + vbuf[slot, h, j] * p[j:j + 1, :]
        outs.append((jnp.sum(acc, axis=1, keepdims=True) + pn * v_cols[h]) / l)
    o_ref[...] = _cols_to_row(outs)


def _moba_attend(page_table, sel, q_row, k_row, v_row, kt_pages, vt_pages):
    batch = page_table.shape[0]
    row = lambda: pl.BlockSpec((None, 1, MOBA_W), lambda b, pt, sl: (b, 0, 0))
    grid_spec = pltpu.PrefetchScalarGridSpec(
        num_scalar_prefetch=2,
        grid=(batch,),
        in_specs=[row(), row(), row(), pl.BlockSpec(memory_space=pl.ANY), pl.BlockSpec(memory_space=pl.ANY)],
        out_specs=row(),
        scratch_shapes=[pltpu.VMEM((2, HEADS, SEL_PAGES, MOBA_DH, PAGE), F32),
                        pltpu.VMEM((2, HEADS, SEL_PAGES, MOBA_DH, PAGE), F32),
                        pltpu.SemaphoreType.DMA((2, 2))],
    )
    return pl.pallas_call(
        _moba_attend_kernel,
        out_shape=jax.ShapeDtypeStruct((batch, 1, MOBA_W), F32),
        grid_spec=grid_spec,
        compiler_params=pltpu.CompilerParams(dimension_semantics=("arbitrary",), vmem_limit_bytes=VMEM_LIMIT),
        name="moba_attend",
    )(page_table, sel, q_row, k_row, v_row, kt_pages, vt_pages)


def _mla_decode_kernel(ppc, nch, pt_ref, qn_ref, qr_ref, cnew_ref, pnew_ref, wukt_ref, wuv_ref, gkn_ref, gkr_ref,
                       cos_ref, sin_ref, cosn_ref, sinn_ref, ckv_hbm, kpe_hbm, o_ref,
                       cbuf, pbuf, sem, a_ref, m_ref, l_ref, acc_ref):
    b = pl.program_id(0)
    c = pl.program_id(1)
    step = b * nch + c
    slot = step % 2
    lhs_rows = HEADS * NOPE
    tk = ppc * PAGE

    def copies(seq, chunk, slot):
        out = []
        for k in range(ppc):
            page = 0 if seq is None else pt_ref[seq, chunk * ppc + k]
            out.append(pltpu.make_async_copy(ckv_hbm.at[page], cbuf.at[slot, k], sem.at[0, slot]))
            out.append(pltpu.make_async_copy(kpe_hbm.at[page], pbuf.at[slot, k], sem.at[1, slot]))
        return out

    @pl.when(step == 0)
    def _prime():
        for cp in copies(0, 0, 0):
            cp.start()
        a_ref[0:lhs_rows, :] = wukt_ref[...]

    last_chunk = c == nch - 1

    @pl.when(step + 1 < pl.num_programs(0) * nch)
    def _prefetch_next():
        for cp in copies(jnp.where(last_chunk, b + 1, b), jnp.where(last_chunk, 0, c + 1), 1 - slot):
            cp.start()

    @pl.when(c == 0)
    def _start_sequence():
        rowh = lax.broadcasted_iota(jnp.int32, (HEADS, MLA_W), 0)
        colh = lax.broadcasted_iota(jnp.int32, (HEADS, MLA_W), 1)
        qg = jnp.where(colh // NOPE == rowh, qn_ref[...] * gkn_ref[...], 0.0).astype(BF16)
        qlat = _dot(qg, wukt_ref[...])
        a_ref[lhs_rows:lhs_rows + 16, :] = jnp.concatenate([qlat, jnp.zeros_like(qlat)], axis=0).astype(BF16)
        m_ref[...] = jnp.full(m_ref.shape, NEG, F32)
        l_ref[...] = jnp.zeros(l_ref.shape, F32)
        acc_ref[...] = jnp.zeros(acc_ref.shape, F32)

    for cp in copies(None, None, slot):
        cp.wait()

    qr = qr_ref[...].astype(BF16)
    gkr = gkr_ref[...]

    def scores(cb, kpt, cos, sin):
        kt = _dot_nt(a_ref[...], cb)
        n_keys = cb.shape[0]
        kn = kt[0:lhs_rows, :]
        ssn = jnp.sum((kn * kn).reshape(HEADS, NOPE, n_keys), axis=1)
        sn = kt[lhs_rows:lhs_rows + HEADS, :]
        n2 = jnp.sum(kpt * kpt, axis=0, keepdims=True)
        y = kpt * gkr
        y1, y2 = y[0:ROPE // 2, :], y[ROPE // 2:ROPE, :]
        kr = jnp.concatenate([y1 * cos - y2 * sin, y2 * cos + y1 * sin], axis=0)
        sr = _dot(qr, kr.astype(BF16))
        return (sn + sr) * lax.rsqrt((ssn + n2) * (1.0 / QK_DIM) + EPS)

    def accumulate(s, cb):
        m = m_ref[...]
        m2 = jnp.maximum(m, jnp.max(s, axis=1, keepdims=True))
        a = jnp.exp(m - m2)
        p = jnp.exp(s - m2)
        l_ref[...] = a * l_ref[...] + jnp.sum(p, axis=1, keepdims=True)
        acc_ref[...] = a * acc_ref[...] + _dot(p.astype(BF16), cb)
        m_ref[...] = m2

    sub = min(MLA_SCORE_PAGES, ppc)
    cos_c, sin_c = cos_ref[c], sin_ref[c]
    cbs, ss = [], []
    for t in range(ppc // sub):
        cb = cbuf[slot, t * sub:(t + 1) * sub].reshape(sub * PAGE, KV_LORA).astype(BF16)
        kpt = jnp.concatenate([pbuf[slot, k] for k in range(t * sub, (t + 1) * sub)], axis=1)
        lo, hi = t * sub * PAGE, (t + 1) * sub * PAGE
        ss.append(scores(cb, kpt, cos_c[:, lo:hi], sin_c[:, lo:hi]))
        cbs.append(cb)
    accumulate(jnp.concatenate(ss, axis=1), jnp.concatenate(cbs, axis=0))

    @pl.when(last_chunk)
    def _finish_sequence():
        rowi = lax.broadcasted_iota(jnp.int32, (PAGE, KV_LORA), 0)
        lanei = lax.broadcasted_iota(jnp.int32, (ROPE, PAGE), 1)
        cb = jnp.where(rowi == 0, jnp.broadcast_to(cnew_ref[...], (PAGE, KV_LORA)), 0.0).astype(BF16)
        kpt = jnp.where(lanei == 0, jnp.broadcast_to(pnew_ref[...], (ROPE, PAGE)), 0.0)
        key_ok = lax.broadcasted_iota(jnp.int32, (HEADS, PAGE), 1) == 0
        accumulate(jnp.where(key_ok, scores(cb, kpt, cosn_ref[...], sinn_ref[...]), NEG), cb)
        ol = (acc_ref[...] / l_ref[...]).astype(BF16)
        res = _dot(ol, wuv_ref[...])
        rowh = lax.broadcasted_iota(jnp.int32, (HEADS, MLA_W), 0)
        colh = lax.broadcasted_iota(jnp.int32, (HEADS, MLA_W), 1)
        o_ref[...] = jnp.sum(jnp.where(colh // V_DIM == rowh, res, 0.0), axis=0, keepdims=True)


def _mla_decode(page_table, qn, qr, c_new, p_new, wukt, wuv, gkn, gkr_col, cos_t, sin_t, cos_n, sin_n,
                ckv_pages, kpet_pages):
    batch, n_pages = page_table.shape
    ppc = min(MLA_PAGES_PER_STEP, n_pages)
    nch = n_pages // ppc
    seq_spec = lambda shape: pl.BlockSpec((None,) + shape, lambda b, c, pt: (b,) + (0,) * len(shape))
    const = lambda arr: pl.BlockSpec(arr.shape, lambda b, c, pt: (0,) * arr.ndim)
    grid_spec = pltpu.PrefetchScalarGridSpec(
        num_scalar_prefetch=1,
        grid=(batch, nch),
        in_specs=[seq_spec((1, MLA_W)), seq_spec((HEADS, ROPE)), seq_spec((1, KV_LORA)), seq_spec((ROPE, 1)),
                  const(wukt), const(wuv), const(gkn), const(gkr_col), const(cos_t), const(sin_t), const(cos_n),
                  const(sin_n), pl.BlockSpec(memory_space=pl.ANY), pl.BlockSpec(memory_space=pl.ANY)],
        out_specs=seq_spec((1, MLA_W)),
        scratch_shapes=[pltpu.VMEM((2, ppc, PAGE, KV_LORA), F32), pltpu.VMEM((2, ppc, ROPE, PAGE), F32),
                        pltpu.SemaphoreType.DMA((2, 2)),
                        pltpu.VMEM((HEADS * NOPE + 16, KV_LORA), BF16), pltpu.VMEM((HEADS, 1), F32),
                        pltpu.VMEM((HEADS, 1), F32), pltpu.VMEM((HEADS, KV_LORA), F32)],
    )
    return pl.pallas_call(
        functools.partial(_mla_decode_kernel, ppc, nch),
        out_shape=jax.ShapeDtypeStruct((batch, 1, MLA_W), F32),
        grid_spec=grid_spec,
        compiler_params=pltpu.CompilerParams(dimension_semantics=("arbitrary", "arbitrary"),
                                             vmem_limit_bytes=VMEM_LIMIT),
        name="mla_decode",
    )(page_table, qn, qr, c_new, p_new, wukt, wuv, gkn, gkr_col, cos_t, sin_t, cos_n, sin_n, ckv_pages, kpet_pages)


def _rope_angles(pos, half):
    inv = np.power(np.float64(ROPE_THETA), -np.arange(half, dtype=np.float64) / half)
    return np.asarray(pos, np.float64)[:, None] * inv[None, :]


def _row_tables(pos, half):
    ang = _rope_angles(pos, half)
    k = np.arange(LANES) % (2 * half)
    cos = np.cos(ang)[:, k % half]
    sin = np.sin(ang)[:, k % half] * np.where(k < half, -1.0, 1.0).astype(np.float32)[None, :]
    return jnp.asarray(cos, F32), jnp.asarray(sin, F32)


def _block_ones(rows_per_group, groups, cols):
    m = np.zeros((groups * rows_per_group, cols), np.float32)
    m[np.arange(groups * rows_per_group), np.arange(groups * rows_per_group) // rows_per_group] = 1.0
    return m


def kernel(x_prompt, x_sample, cache_moba_k, cache_moba_v, cache_mla_ckv, cache_mla_kpe, page_table, norm_gain, w_in, moba_q_gain, moba_k_gain, mla_q_lat_gain, w_uq, mla_q_gain, mla_kv_lat_gain, w_uk, w_uv, mla_k_gain, w_branch_a, w_branch_b, w_out):
    batch, seq, _ = x_prompt.shape
    dec_batch, dec_seq, _ = x_sample.shape
    depth = norm_gain.shape[0]
    assert depth == 1 and dec_seq == 1
    n_pages = page_table.shape[1]
    past = n_pages * PAGE
    assert seq % ATTN_TILE == 0 and seq // MOBA_BLOCK <= 8 and past % MOBA_BLOCK == 0
    assert past // MOBA_BLOCK >= MOBA_TOPK
    assert n_pages % min(PAGES_PER_STEP, n_pages) == 0 and n_pages % min(MLA_PAGES_PER_STEP, n_pages) == 0

    w = w_in[0]
    bf = lambda a: a.astype(BF16)
    o = 4 * MOBA_W
    wqk, wv, wg = bf(w[:, 0:2 * MOBA_W]), bf(w[:, 2 * MOBA_W:3 * MOBA_W]), bf(w[:, 3 * MOBA_W:o])
    w2 = bf(jnp.pad(w[:, o:o + Q_LORA + KV_LORA + ROPE], ((0, 0), (0, LANES - ROPE))))
    o += Q_LORA + KV_LORA + ROPE
    w3 = bf(w[:, o:o + MLA_W + 2 * D_MODEL])
    wuqn = bf(w_uq[0][:, :, :NOPE].reshape(Q_LORA, HEADS * NOPE))
    wuqr = bf(w_uq[0][:, :, NOPE:].reshape(Q_LORA, HEADS * ROPE))
    wuk = bf(w_uk[0].reshape(KV_LORA, HEADS * NOPE))
    wuv = bf(w_uv[0].reshape(KV_LORA, MLA_W))
    tile8 = lambda g: jnp.tile(g, HEADS)[None, :].astype(F32)
    gains = [tile8(moba_q_gain[0]), tile8(moba_k_gain[0]), mla_q_lat_gain[0][None, :], mla_kv_lat_gain[0][None, :],
             tile8(mla_q_gain[0][:NOPE]), tile8(mla_q_gain[0][NOPE:]), tile8(mla_k_gain[0][:NOPE]),
             tile8(mla_k_gain[0][NOPE:])]
    place = lambda width, off: np.eye(HEADS * LANES, dtype=np.float32)[
        (np.arange(HEADS * width) // width) * LANES + off + np.arange(HEADS * width) % width]
    helpers = [
        jnp.asarray(_block_ones(MOBA_DH, HEADS, HEADS) @ _block_ones(MOBA_DH, HEADS, HEADS).T, BF16),
        jnp.asarray(_block_ones(NOPE, HEADS, LANES), BF16),
        jnp.asarray(_block_ones(ROPE, HEADS, LANES), BF16),
        jnp.asarray(_block_ones(NOPE, HEADS, LANES).T, BF16),
        jnp.asarray(_block_ones(ROPE, HEADS, LANES).T, BF16),
        jnp.asarray(np.concatenate([np.tile(np.eye(ROPE, dtype=np.float32), (1, HEADS)),
                                    np.zeros((LANES - ROPE, HEADS * ROPE), np.float32)], axis=0), BF16),
        jnp.asarray(place(NOPE, 0), BF16),
        jnp.asarray(place(ROPE, NOPE), BF16),
    ]
    consts_p = [norm_gain[0][None, :], wqk, wv.T, wg, w2, w3, wuqn, wuqr, wuk, wuv.T] + gains + helpers
    consts_s = [norm_gain[0][None, :], wqk, wv, wg, w2, w3, wuqn, wuqr, wuk, wuv] + gains + helpers
    assert len(consts_p) - len(helpers) == N_FRONT_WEIGHTS
    pos_p = np.arange(seq)
    pos_s = np.full((dec_batch,), past)
    tables_p = _row_tables(pos_p, MOBA_DH // 2) + _row_tables(pos_p, ROPE // 2)
    tables_s = _row_tables(pos_s, MOBA_DH // 2) + _row_tables(pos_s, ROPE // 2)
    wa = bf(w_branch_a[0])
    wb = bf(w_branch_b[0])
    wo = bf(w_out[0])

    xp = x_prompt.reshape(batch * seq, D_MODEL)
    (qa, kat, vat, ga, ckv, kpe, gb, g1, g2, qcat, kcat, vmt) = _front(xp, consts_p, tables_p, True, seq)
    xs = x_sample.reshape(dec_batch, D_MODEL)
    (qa_s, ka_s, va_s, ga_s, ckv_s, kpe_s, gb_s, g1_s, g2_s, qn_s, qr_s) = _front(xs, consts_s, tables_s, False, 1)
    kt_pages = jnp.transpose(cache_moba_k[0], (0, 2, 3, 1))
    vt_pages = jnp.transpose(cache_moba_v[0], (0, 2, 3, 1))
    ckv_pages = cache_mla_ckv[0]
    kpet_pages = jnp.transpose(cache_mla_kpe[0], (0, 2, 1))

    sel_ppc = min(PAGES_PER_STEP, n_pages)
    sel_nch = n_pages // sel_ppc
    nq = seq // ATTN_TILE
    loop_iterations = batch * nq * (nq + 1) // 2
    chunks = dec_batch * sel_nch
    chunks_a = min(chunks, (loop_iterations // sel_nch) * sel_nch)
    assert chunks - chunks_a <= loop_iterations, "paged decode too large to hide under the prompt attention"
    oa, sel_a = _moba_prompt(qa, kat, vat, ga, batch, seq, page_table, qa_s, kt_pages, (0, chunks_a, sel_ppc, sel_nch))
    ob, sel_b = _mla_prompt(qcat, kcat, vmt, gb, batch, seq, page_table, qa_s, kt_pages,
                            (chunks_a, chunks - chunks_a, sel_ppc, sel_nch))
    y_prompt = _merge(xp, oa, ob, (), g1, g2, wa, wb, wo).reshape(batch, seq, D_MODEL)

    row3 = lambda a: a.reshape(dec_batch, 1, MOBA_W)
    first_b = (jnp.arange(dec_batch) >= chunks_a // sel_nch)[:, None, None]
    sel = jnp.where(first_b, sel_b[:dec_batch], sel_a[:dec_batch])[:, :, :MOBA_TOPK].reshape(dec_batch, HEADS * MOBA_TOPK)
    oa_s = _moba_attend(page_table, sel, row3(qa_s), row3(ka_s), row3(va_s), kt_pages, vt_pages).reshape(dec_batch, MOBA_W)

    ppc = min(MLA_PAGES_PER_STEP, n_pages)
    ang_past = _rope_angles(np.arange(past), ROPE // 2).T
    ang_past = ang_past.reshape(ROPE // 2, n_pages // ppc, ppc * PAGE).transpose(1, 0, 2)
    ang_new = np.broadcast_to(_rope_angles(np.array([past]), ROPE // 2).T, (ROPE // 2, PAGE))
    wukt = jnp.transpose(w_uk[0].reshape(KV_LORA, HEADS * NOPE)).astype(BF16)
    ob_s = _mla_decode(
        page_table, qn_s.reshape(dec_batch, 1, MLA_W), qr_s.reshape(dec_batch, HEADS, ROPE),
        ckv_s.reshape(dec_batch, 1, KV_LORA), kpe_s.reshape(dec_batch, ROPE, 1),
        wukt, wuv, tile8(mla_k_gain[0][:NOPE]), mla_k_gain[0][NOPE:].reshape(ROPE, 1),
        jnp.asarray(np.cos(ang_past), F32), jnp.asarray(np.sin(ang_past), F32),
        jnp.asarray(np.cos(ang_new), F32), jnp.asarray(np.sin(ang_new), F32),
        ckv_pages, kpet_pages).reshape(dec_batch, MLA_W)
    y_sample = _merge(xs, oa_s, ob_s, (ga_s, gb_s), g1_s, g2_s, wa, wb, wo).reshape(dec_batch, 1, D_MODEL)

    p5 = lambda a_t: a_t.reshape(1, batch, HEADS, MOBA_DH, seq).transpose(0, 1, 4, 2, 3)
    s5 = lambda a, w: a.reshape(1, dec_batch, 1, HEADS, w)
    return (y_prompt, y_sample,
            p5(kat), p5(vat), ckv.reshape(1, batch, seq, KV_LORA), kpe.reshape(1, batch, seq, ROPE),
            s5(ka_s, MOBA_DH), s5(va_s, MOBA_DH), ckv_s.reshape(1, dec_batch, 1, KV_LORA),
            kpe_s.reshape(1, dec_batch, 1, ROPE))
```

```python
import functools

import jax
import jax.numpy as jnp
import numpy as np
from jax import lax
from jax.experimental import pallas as pl
from jax.experimental.pallas import tpu as pltpu

F32 = jnp.float32
BF16 = jnp.bfloat16

D_MODEL = 1024
HEADS = 8
MOBA_DH = 64
MOBA_W = HEADS * MOBA_DH
MOBA_BLOCK = 256
MOBA_TOPK = 3
NOPE = 64
ROPE = 32
QK_DIM = NOPE + ROPE
V_DIM = 64
MLA_W = HEADS * V_DIM
Q_LORA = 384
KV_LORA = 256
PAGE = 128
ROPE_THETA = 10000.0
EPS = 1e-6
NEG = -1e30

LANES = 128
V7X_VMEM_BYTES = 64 * 1024 * 1024
VMEM_LIMIT = V7X_VMEM_BYTES - 8 * 1024 * 1024

TOKEN_TILE = 256
ATTN_TILE = MOBA_BLOCK
PAGES_PER_STEP = 16
MLA_PAGES_PER_STEP = 64
MLA_SCORE_PAGES = 16


def _dot(a, b):
    return jnp.dot(a, b, preferred_element_type=F32)


def _dot_nt(a, b):
    return lax.dot_general(a, b, (((1,), (1,)), ((), ())), preferred_element_type=F32)


def _dot_split(x, w):
    hi = x.astype(BF16)
    lo = (x - hi.astype(F32)).astype(BF16)
    return _dot(hi, w) + _dot(lo, w)


def _sigmoid(x):
    return 1.0 / (1.0 + jnp.exp(-x))


def _silu(x):
    return x * _sigmoid(x)


def _rope_rows(y, cos, sin_signed, half):
    lane = lax.broadcasted_iota(jnp.int32, (y.shape[0], LANES), 1)
    first = (lane % (2 * half)) < half
    outs = []
    for c in range(y.shape[1] // LANES):
        yc = y[:, c * LANES:(c + 1) * LANES]
        up = pltpu.roll(yc, LANES - half, axis=1)
        dn = pltpu.roll(yc, half, axis=1)
        outs.append(yc * cos + jnp.where(first, up, dn) * sin_signed)
    return jnp.concatenate(outs, axis=1)


def _front_kernel(prompt, x_ref, ng_ref, wqk_ref, wv_ref, wg_ref, w2_ref, w3_ref, wuqn_ref, wuqr_ref, wuk_ref, wuv_ref,
                  gq_ref, gk_ref, gcq_ref, gckv_ref, gqn_ref, gqr_ref, gkn_ref, gkr_ref,
                  cos64_ref, sin64_ref, cos32_ref, sin32_ref,
                  g64_ref, s64_ref, s32_ref, e64_ref, e32_ref, t32_ref, pn_ref, pr_ref,
                  qa_ref, ka_ref, v_ref, ga_ref, ckv_ref, kpe_ref, gb_ref, g1_ref, g2_ref, *mode_refs):
    x = x_ref[...]
    h = (x * lax.rsqrt(jnp.mean(x * x, axis=-1, keepdims=True) + EPS) * ng_ref[...]).astype(BF16)

    z1 = _dot(h, wqk_ref[...])
    g64 = g64_ref[...]
    cos64, sin64 = cos64_ref[...], sin64_ref[...]

    def head_norm64(z, gain):
        ss = _dot_split(z * z, g64)
        return z * lax.rsqrt(ss * (1.0 / MOBA_DH) + EPS) * gain

    qa = _rope_rows(head_norm64(z1[:, 0:MOBA_W], gq_ref[...]), cos64, sin64, MOBA_DH // 2)
    qa_ref[...] = (qa * MOBA_DH ** -0.5).astype(qa_ref.dtype)
    ka = _rope_rows(head_norm64(z1[:, MOBA_W:2 * MOBA_W], gk_ref[...]), cos64, sin64, MOBA_DH // 2)
    if prompt:
        ka_ref[...] = jnp.transpose(ka)
        v_ref[...] = _dot_nt(wv_ref[...], h)
    else:
        ka_ref[...] = ka
        v_ref[...] = _dot(h, wv_ref[...])
    ga_ref[...] = _dot(h, wg_ref[...])

    z2 = _dot(h, w2_ref[...])
    cq = z2[:, 0:Q_LORA]
    cqn = (cq * lax.rsqrt(jnp.mean(cq * cq, axis=-1, keepdims=True) + EPS) * gcq_ref[...]).astype(BF16)
    qn = _dot(cqn, wuqn_ref[...])
    qr = _dot(cqn, wuqr_ref[...])
    s64, s32, e64, e32 = s64_ref[...], s32_ref[...], e64_ref[...], e32_ref[...]
    ss = _dot_split(qn * qn, s64) + _dot_split(qr * qr, s32)
    r = lax.rsqrt(ss * (1.0 / QK_DIM) + EPS)
    scale = QK_DIM ** -0.5
    cos32, sin32 = cos32_ref[...], sin32_ref[...]
    qn_n = qn * _dot_split(r, e64) * (gqn_ref[...] * scale)
    qr_n = _rope_rows(qr * _dot_split(r, e32) * (gqr_ref[...] * scale), cos32, sin32, ROPE // 2)

    ckv_raw = z2[:, Q_LORA:Q_LORA + KV_LORA]
    ckv = ckv_raw * lax.rsqrt(jnp.mean(ckv_raw * ckv_raw, axis=-1, keepdims=True) + EPS) * gckv_ref[...]
    ckv_ref[...] = ckv
    kpe_p = z2[:, Q_LORA + KV_LORA:Q_LORA + KV_LORA + LANES]
    kpe_ref[...] = kpe_p[:, 0:ROPE]

    z3 = _dot(h, w3_ref[...])
    gb_ref[...] = z3[:, 0:MLA_W]
    g1_ref[...] = z3[:, MLA_W:MLA_W + D_MODEL]
    g2_ref[...] = z3[:, MLA_W + D_MODEL:MLA_W + 2 * D_MODEL]

    if not prompt:
        qn_ref, qr_ref = mode_refs
        qn_ref[...] = qn_n
        qr_ref[...] = qr_n
        return
    qcat_ref, kcat_ref, vmt_ref = mode_refs
    pn, pr = pn_ref[...], pr_ref[...]
    qcat_ref[...] = (_dot(qn_n.astype(BF16), pn) + _dot(qr_n.astype(BF16), pr)).astype(BF16)
    cb = ckv.astype(BF16)
    kn = _dot(cb, wuk_ref[...])
    ssk = _dot_split(kn * kn, s64) + jnp.sum(kpe_p * kpe_p, axis=-1, keepdims=True)
    rk = lax.rsqrt(ssk * (1.0 / QK_DIM) + EPS)
    kn_n = (kn * _dot_split(rk, e64) * gkn_ref[...]).astype(BF16)
    kt = _dot_split(kpe_p, t32_ref[...]) * gkr_ref[...]
    kr_n = (_rope_rows(kt, cos32, sin32, ROPE // 2) * _dot_split(rk, e32)).astype(BF16)
    kcat_ref[...] = (_dot(kn_n, pn) + _dot(kr_n, pr)).astype(BF16)
    vmt_ref[...] = _dot_nt(wuv_ref[...], cb).astype(BF16)


def _const_spec(shape):
    nd = len(shape)
    return pl.BlockSpec(shape, lambda *_: (0,) * nd, pipeline_mode=pl.Buffered(1))


N_FRONT_WEIGHTS = 18


def _front(x, consts, tables, prompt, seq):
    t = x.shape[0]
    tm = min(TOKEN_TILE, t)
    n_tab = tables[0].shape[0] // tm
    row = lambda w: pl.BlockSpec((tm, w), lambda i: (i, 0))
    tab = lambda: pl.BlockSpec((tm, LANES), lambda i: (i % n_tab, 0))
    in_specs = [row(D_MODEL)] + [_const_spec(c.shape) for c in consts[:N_FRONT_WEIGHTS]] + [tab() for _ in range(4)] + \
               [_const_spec(c.shape) for c in consts[N_FRONT_WEIGHTS:]]
    q_dtype = BF16 if prompt else F32
    rows = lambda w, d: (jax.ShapeDtypeStruct((t, w), d), row(w))
    if prompt:
        nq = seq // tm
        cols = lambda d: (jax.ShapeDtypeStruct((t // seq, MOBA_W, seq), d),
                          pl.BlockSpec((None, MOBA_W, tm), lambda i: (i // nq, 0, i % nq)))
        kv_out = cols(F32)
        mode_outs = [rows(HEADS * LANES, BF16), rows(HEADS * LANES, BF16), cols(BF16)]
    else:
        kv_out = rows(MOBA_W, F32)
        mode_outs = [rows(MLA_W, F32), rows(HEADS * ROPE, F32)]
    outs = [rows(MOBA_W, q_dtype), kv_out, kv_out, rows(MOBA_W, F32), rows(KV_LORA, F32), rows(ROPE, F32),
            rows(MLA_W, F32), rows(D_MODEL, F32), rows(D_MODEL, F32)] + mode_outs
    return pl.pallas_call(
        functools.partial(_front_kernel, prompt),
        out_shape=[o[0] for o in outs],
        grid=(t // tm,),
        in_specs=in_specs,
        out_specs=[o[1] for o in outs],
        compiler_params=pltpu.CompilerParams(dimension_semantics=("arbitrary",), vmem_limit_bytes=VMEM_LIMIT),
        name="front_prompt" if prompt else "front_sample",
    )(x, *consts[:N_FRONT_WEIGHTS], *tables, *consts[N_FRONT_WEIGHTS:])


def _flash_heads(i, scores_t, values_t, side_work, m_ref, l_ref, acc_ref, s_ref, p_ref):
    m_ref[...] = jnp.full(m_ref.shape, NEG, F32)
    l_ref[...] = jnp.zeros(l_ref.shape, F32)
    acc_ref[...] = jnp.zeros(acc_ref.shape, F32)
    key = lax.broadcasted_iota(jnp.int32, (ATTN_TILE, ATTN_TILE), 0)
    qry = lax.broadcasted_iota(jnp.int32, (ATTN_TILE, ATTN_TILE), 1)
    causal = key <= qry

    def body(n, carry):
        side_work(n)
        keep = jnp.logical_or(causal, n < i)
        for h in range(HEADS):
            s_ref[h] = scores_t(h, n)
        for h in range(HEADS):
            s = jnp.where(keep, s_ref[h], NEG)
            m = m_ref[h]
            m2 = jnp.maximum(m, jnp.max(s, axis=0, keepdims=True))
            a = jnp.exp(m - m2)
            p = jnp.exp(s - m2)
            l_ref[h] = a * l_ref[h] + jnp.sum(p, axis=0, keepdims=True)
            acc_ref[h] = a * acc_ref[h]
            p_ref[h] = p.astype(BF16)
            m_ref[h] = m2
        for h in range(HEADS):
            acc_ref[h] = acc_ref[h] + _dot(values_t(h, n), p_ref[h])
        return carry

    lax.fori_loop(0, i + 1, body, 0)


def _gated_heads_out(acc_ref, l_ref, g_ref, o_ref):
    o_t = jnp.concatenate([acc_ref[h] / l_ref[h] for h in range(HEADS)], axis=0)
    o_ref[...] = (jnp.transpose(o_t) * _silu(g_ref[...])).astype(o_ref.dtype)


def _flash_scratch():
    return [pltpu.VMEM((HEADS, 1, ATTN_TILE), F32), pltpu.VMEM((HEADS, 1, ATTN_TILE), F32),
            pltpu.VMEM((HEADS, V_DIM, ATTN_TILE), F32), pltpu.VMEM((HEADS, ATTN_TILE, ATTN_TILE), F32),
            pltpu.VMEM((HEADS, ATTN_TILE, ATTN_TILE), BF16)]


def _key_tile(n):
    return pl.ds(pl.multiple_of(n * ATTN_TILE, ATTN_TILE), ATTN_TILE)


def _moba_prompt_kernel(nb, topk, select, pt_ref, q_ref, kt_ref, vt_ref, g_ref, qs_ref, kt_hbm, o_ref, sel_ref,
                        kaug_ref, vtb_ref, bmh_ref, bml_ref, qaug_ref, *refs):
    i = pl.program_id(1)
    flash_refs, select_refs = refs[:5], refs[5:]
    lane = lax.broadcasted_iota(jnp.int32, (ATTN_TILE, LANES), 1)

    @pl.when(i == 0)
    def _prepare_sequence():
        for n in range(nb):
            vtb_ref[n] = vt_ref[:, n * MOBA_BLOCK:(n + 1) * MOBA_BLOCK].astype(BF16)
        k_blocks = [jnp.transpose(kt_ref[:, n * MOBA_BLOCK:(n + 1) * MOBA_BLOCK]) for n in range(nb)]
        means = [jnp.mean(k_blocks[n], axis=0, keepdims=True) if n < nb
                 else jnp.zeros((1, MOBA_W), F32) for n in range(8)]
        km = jnp.concatenate(means, axis=0)
        col = lax.broadcasted_iota(jnp.int32, (8, MOBA_W), 1)
        per_head = [jnp.where(col // MOBA_DH == h, km, 0.0) for h in range(HEADS)]
        bm = jnp.concatenate(per_head + per_head, axis=0)
        hi = bm.astype(BF16)
        bmh_ref[...] = hi
        bml_ref[...] = (bm - hi.astype(F32)).astype(BF16)
        for n in range(nb):
            for h in range(HEADS):
                p, half = divmod(h, 2)
                kp = k_blocks[n][:, p * LANES:(p + 1) * LANES]
                keep = (lane < MOBA_DH) if half == 0 else (lane >= MOBA_DH)
                hot = (MOBA_DH + 8 * h + n) if half == 0 else (8 * h + n)
                kaug_ref[h, n * MOBA_BLOCK:(n + 1) * MOBA_BLOCK, :] = jnp.where(
                    keep, kp, jnp.where(lane == hot, 1.0, 0.0)).astype(BF16)

    q = q_ref[...]
    gate = _dot_nt(q, bmh_ref[...]) + _dot_nt(q, bml_ref[...])
    nl = lane % 8
    valid = nl < i
    gm = jnp.where(valid, gate, NEG)
    rank = jnp.zeros(gm.shape, F32)
    for s in range(1, 8):
        below = pltpu.roll(gm, s, axis=1)
        above = pltpu.roll(gm, LANES - s, axis=1)
        rank = rank + jnp.where((nl >= s) & (below >= gm), 1.0, 0.0)
        rank = rank + jnp.where((nl + s < 8) & (above > gm), 1.0, 0.0)
    attend = (valid & (rank < topk)) | (nl == i)
    bias = jnp.where(attend, 0.0, NEG)

    for h in range(HEADS):
        p, half = divmod(h, 2)
        qp = q[:, p * LANES:(p + 1) * LANES].astype(F32)
        if half == 0:
            q_aug = jnp.where(lane < MOBA_DH, qp, jnp.where(lane // 8 == 8 + h, bias, 0.0))
        else:
            q_aug = jnp.where(lane >= MOBA_DH, qp, jnp.where(lane // 8 == h, bias, 0.0))
        qaug_ref[h] = q_aug.astype(BF16)

    def scores_t(h, n):
        return _dot_nt(kaug_ref[h, _key_tile(n), :], qaug_ref[h])

    def values_t(h, n):
        return vtb_ref[n, h * MOBA_DH:(h + 1) * MOBA_DH, :]

    def side_work(n):
        _select_step(_loop_iteration(nb, n), *select, pt_ref, qs_ref, kt_hbm, sel_ref, *select_refs)

    _flash_heads(i, scores_t, values_t, side_work, *flash_refs)
    _gated_heads_out(flash_refs[2], flash_refs[1], g_ref, o_ref)


def _loop_iteration(nq, n):
    b, i = pl.program_id(0), pl.program_id(1)
    return b * (nq * (nq + 1) // 2) + (i * (i + 1)) // 2 + n


def _select_operands(page_table, qs, kt_pages, select):
    _, _, ppc, _ = select
    dec_batch = qs.shape[0]
    in_specs = [pl.BlockSpec(qs.shape, lambda b, i, pt: (0, 0)), pl.BlockSpec(memory_space=pl.ANY)]
    out_shape = jax.ShapeDtypeStruct((dec_batch + 1, HEADS, LANES), jnp.int32)
    out_spec = pl.BlockSpec((dec_batch + 1, HEADS, LANES), lambda b, i, pt: (0, 0, 0))
    return in_specs, out_shape, out_spec, _select_scratch(ppc)


def _moba_prompt(qa, kat, vat, ga, batch, seq, page_table, qs, kt_pages, select):
    nq = seq // ATTN_TILE
    nb = seq // MOBA_BLOCK
    tile = lambda w: pl.BlockSpec((ATTN_TILE, w), lambda b, i, pt: (b * nq + i, 0))
    whole_t = pl.BlockSpec((None, MOBA_W, seq), lambda b, i, pt: (b, 0, 0))
    sel_in, sel_shape, sel_spec, sel_scratch = _select_operands(page_table, qs, kt_pages, select)
    grid_spec = pltpu.PrefetchScalarGridSpec(
        num_scalar_prefetch=1,
        grid=(batch, nq),
        in_specs=[tile(MOBA_W), whole_t, whole_t, tile(MOBA_W)] + sel_in,
        out_specs=[tile(MOBA_W), sel_spec],
        scratch_shapes=[pltpu.VMEM((HEADS, seq, LANES), BF16), pltpu.VMEM((nb, MOBA_W, MOBA_BLOCK), BF16),
                        pltpu.VMEM((LANES, MOBA_W), BF16), pltpu.VMEM((LANES, MOBA_W), BF16),
                        pltpu.VMEM((HEADS, ATTN_TILE, LANES), BF16)] + _flash_scratch() + sel_scratch,
    )
    return pl.pallas_call(
        functools.partial(_moba_prompt_kernel, nb, min(MOBA_TOPK, nb), select),
        out_shape=[jax.ShapeDtypeStruct((batch * seq, MOBA_W), BF16), sel_shape],
        grid_spec=grid_spec,
        compiler_params=pltpu.CompilerParams(dimension_semantics=("arbitrary", "arbitrary"),
                                             vmem_limit_bytes=VMEM_LIMIT),
        name="moba_prompt",
    )(page_table, qa, kat, vat, ga, qs, kt_pages)


def _mla_prompt_kernel(nt, select, pt_ref, q_ref, k_ref, vt_ref, g_ref, qs_ref, kt_hbm, o_ref, sel_ref, vtb_ref, *refs):
    i = pl.program_id(1)
    flash_refs, select_refs = refs[:5], refs[5:]

    @pl.when(i == 0)
    def _prepare_sequence():
        for n in range(nt):
            vtb_ref[n] = vt_ref[:, n * ATTN_TILE:(n + 1) * ATTN_TILE]

    def scores_t(h, n):
        return _dot_nt(k_ref[_key_tile(n), h * LANES:(h + 1) * LANES], q_ref[:, h * LANES:(h + 1) * LANES])

    def values_t(h, n):
        return vtb_ref[n, h * V_DIM:(h + 1) * V_DIM, :]

    def side_work(n):
        _select_step(_loop_iteration(nt, n), *select, pt_ref, qs_ref, kt_hbm, sel_ref, *select_refs)

    _flash_heads(i, scores_t, values_t, side_work, *flash_refs)
    _gated_heads_out(flash_refs[2], flash_refs[1], g_ref, o_ref)


def _mla_prompt(qcat, kcat, vmt, gb, batch, seq, page_table, qs, kt_pages, select):
    nq = seq // ATTN_TILE
    tile = lambda w: pl.BlockSpec((ATTN_TILE, w), lambda b, i, pt: (b * nq + i, 0))
    whole = lambda w: pl.BlockSpec((seq, w), lambda b, i, pt: (b, 0))
    whole_t = pl.BlockSpec((None, MLA_W, seq), lambda b, i, pt: (b, 0, 0))
    sel_in, sel_shape, sel_spec, sel_scratch = _select_operands(page_table, qs, kt_pages, select)
    grid_spec = pltpu.PrefetchScalarGridSpec(
        num_scalar_prefetch=1,
        grid=(batch, nq),
        in_specs=[tile(HEADS * LANES), whole(HEADS * LANES), whole_t, tile(MLA_W)] + sel_in,
        out_specs=[tile(MLA_W), sel_spec],
        scratch_shapes=[pltpu.VMEM((nq, MLA_W, ATTN_TILE), BF16)] + _flash_scratch() + sel_scratch,
    )
    return pl.pallas_call(
        functools.partial(_mla_prompt_kernel, nq, select),
        out_shape=[jax.ShapeDtypeStruct((batch * seq, MLA_W), BF16), sel_shape],
        grid_spec=grid_spec,
        compiler_params=pltpu.CompilerParams(dimension_semantics=("arbitrary", "arbitrary"),
                                             vmem_limit_bytes=VMEM_LIMIT),
        name="mla_prompt",
    )(page_table, qcat, kcat, vmt, gb, qs, kt_pages)


def _merge_kernel(n_gates, x_ref, oa_ref, ob_ref, *refs):
    gate_refs = refs[:n_gates]
    g1_ref, g2_ref, wa_ref, wb_ref, wo_ref, y_ref = refs[n_gates:]
    oa, ob = oa_ref[...], ob_ref[...]
    if n_gates:
        oa = (oa * _silu(gate_refs[0][...])).astype(BF16)
        ob = (ob * _silu(gate_refs[1][...])).astype(BF16)
    a = _dot(oa, wa_ref[...])
    b = _dot(ob, wb_ref[...])
    hm = _sigmoid(g1_ref[...]) * a + _sigmoid(g2_ref[...]) * b
    y_ref[...] = x_ref[...] + _dot(hm.astype(BF16), wo_ref[...])


def _merge(x, oa, ob, branch_gates, g1, g2, wa, wb, wo):
    t = x.shape[0]
    tm = min(TOKEN_TILE, t)
    row = lambda w: pl.BlockSpec((tm, w), lambda i: (i, 0))
    return pl.pallas_call(
        functools.partial(_merge_kernel, len(branch_gates)),
        out_shape=jax.ShapeDtypeStruct((t, D_MODEL), F32),
        grid=(t // tm,),
        in_specs=[row(D_MODEL), row(MOBA_W), row(MLA_W)] + [row(g.shape[1]) for g in branch_gates] +
                 [row(D_MODEL), row(D_MODEL), _const_spec(wa.shape), _const_spec(wb.shape), _const_spec(wo.shape)],
        out_specs=row(D_MODEL),
        compiler_params=pltpu.CompilerParams(dimension_semantics=("arbitrary",), vmem_limit_bytes=VMEM_LIMIT),
        name="merge_sample" if branch_gates else "merge_prompt",
    )(x, oa, ob, *branch_gates, g1, g2, wa, wb, wo)


def _row_to_cols(row):
    eye = (lax.broadcasted_iota(jnp.int32, (MOBA_DH, MOBA_DH), 0) == lax.broadcasted_iota(jnp.int32, (MOBA_DH, MOBA_DH), 1))
    return [jnp.sum(jnp.where(eye, jnp.broadcast_to(row[:, h * MOBA_DH:(h + 1) * MOBA_DH], (MOBA_DH, MOBA_DH)), 0.0),
                    axis=1, keepdims=True) for h in range(HEADS)]


def _cols_to_row(cols):
    eye = (lax.broadcasted_iota(jnp.int32, (MOBA_DH, MOBA_DH), 0) == lax.broadcasted_iota(jnp.int32, (MOBA_DH, MOBA_DH), 1))
    return jnp.concatenate([jnp.sum(jnp.where(eye, jnp.broadcast_to(c, (MOBA_DH, MOBA_DH)), 0.0), axis=0, keepdims=True)
                            for c in cols], axis=1)


SELECT_SLOTS = 3


def _select_scratch(ppc):
    return [pltpu.VMEM((SELECT_SLOTS, ppc, HEADS, MOBA_DH, PAGE), F32), pltpu.SemaphoreType.DMA((SELECT_SLOTS,)),
            pltpu.VMEM((HEADS, LANES), F32)]


def _select_step(it, first_chunk, n_chunks, ppc, nch, pt_ref, q_ref, kt_hbm, sel_ref, kbuf, sem, gate_ref):
    lane = lax.broadcasted_iota(jnp.int32, (HEADS, LANES), 1)
    low = jnp.float32(-3.0e38)

    def copies(page_of, slot):
        return [pltpu.make_async_copy(kt_hbm.at[page_of(k)], kbuf.at[slot, k], sem.at[slot]) for k in range(ppc)]

    def fetch(j):
        g = first_chunk + j
        seq, chunk = g // nch, g % nch
        for cp in copies(lambda k: pt_ref[seq, chunk * ppc + k], j % SELECT_SLOTS):
            cp.start()

    @pl.when(it == 0)
    def _prime():
        sel_ref[...] = jnp.zeros(sel_ref.shape, jnp.int32)
        gate_ref[...] = jnp.full((HEADS, LANES), low, F32)
        for j in range(n_chunks, SELECT_SLOTS):
            kbuf[j] = jnp.zeros(kbuf.shape[1:], F32)
        for j in range(min(SELECT_SLOTS - 1, n_chunks)):
            fetch(j)

    @pl.when(it + SELECT_SLOTS - 1 < n_chunks)
    def _prefetch():
        fetch(it + SELECT_SLOTS - 1)

    if n_chunks == 0:
        return
    active = it < n_chunks
    slot = it % SELECT_SLOTS

    @pl.when(active)
    def _wait():
        for cp in copies(lambda k: 0, slot):
            cp.wait()

    g = first_chunk + jnp.minimum(it, n_chunks - 1)
    seq, c = g // nch, g % nch
    qb = jnp.stack([jnp.broadcast_to(qc, (MOBA_DH, LANES)) for qc in _row_to_cols(q_ref[pl.ds(seq, 1), :])], axis=0)
    gate = jnp.where(c == 0, low, gate_ref[...])
    blocks_per_step = ppc // 2
    for j in range(blocks_per_step):
        ksum = kbuf[slot, 2 * j] + kbuf[slot, 2 * j + 1]
        part = jnp.sum(ksum * qb, axis=1)
        gj = jnp.sum(part, axis=1, keepdims=True) * (1.0 / MOBA_BLOCK)
        gate = jnp.where(lane == c * blocks_per_step + j, gj, gate)
    gate_ref[...] = gate

    gw = gate
    out = jnp.zeros((HEADS, LANES), jnp.int32)
    for r in range(MOBA_TOPK):
        mx = jnp.max(gw, axis=1, keepdims=True)
        idx = jnp.min(jnp.where(gw == mx, lane, LANES), axis=1, keepdims=True)
        out = jnp.where(lane == r, idx, out)
        gw = jnp.where(lane == idx, low, gw)
    spare_row = sel_ref.shape[0] - 1
    sel_ref[jnp.where(jnp.logical_and(active, c == nch - 1), seq, spare_row)] = out


SEL_PAGES = 2 * MOBA_TOPK


def _moba_attend_kernel(pt_ref, sel_ref, q_ref, kn_ref, vn_ref, kt_hbm, vt_hbm, o_ref, kbuf, vbuf, sem):
    b = pl.program_id(0)
    slot = b % 2

    def copies(seq, slot):
        out = []
        for h in range(HEADS):
            for r in range(MOBA_TOPK):
                blk = 0 if seq is None else sel_ref[seq, h * MOBA_TOPK + r]
                for pg in range(2):
                    page = 0 if seq is None else pt_ref[seq, 2 * blk + pg]
                    out.append(pltpu.make_async_copy(kt_hbm.at[page, h], kbuf.at[slot, h, 2 * r + pg], sem.at[0, slot]))
                    out.append(pltpu.make_async_copy(vt_hbm.at[page, h], vbuf.at[slot, h, 2 * r + pg], sem.at[1, slot]))
        return out

    @pl.when(b == 0)
    def _prime():
        for cp in copies(0, 0):
            cp.start()

    @pl.when(b + 1 < pl.num_programs(0))
    def _prefetch_next():
        for cp in copies(b + 1, 1 - slot):
            cp.start()

    for cp in copies(None, slot):
        cp.wait()

    q_cols = _row_to_cols(q_ref[...])
    k_cols = _row_to_cols(kn_ref[...])
    v_cols = _row_to_cols(vn_ref[...])
    outs = []
    for h in range(HEADS):
        qb = jnp.broadcast_to(q_cols[h], (MOBA_DH, LANES))
        s = jnp.concatenate([jnp.sum(kbuf[slot, h, j] * qb, axis=0, keepdims=True) for j in range(SEL_PAGES)], axis=0)
        sn = jnp.sum(q_cols[h] * k_cols[h], axis=0, keepdims=True)
        m = jnp.maximum(jnp.max(jnp.max(s, axis=1, keepdims=True), axis=0, keepdims=True), sn)
        p = jnp.exp(s - m)
        pn = jnp.exp(sn - m)
        l = jnp.sum(jnp.sum(p, axis=1, keepdims=True), axis=0, keepdims=True) + pn
        acc = jnp.zeros((MOBA_DH, LANES), F32)
        for j in range(SEL_PAGES):
            acc = acc + vbuf[slot, h, j] * p[j:j + 1, :]
        outs.append((jnp.sum(acc, axis=1, keepdims=True) + pn * v_cols[h]) / l)
    o_ref[...] = _cols_to_row(outs)


def _moba_attend(page_table, sel, q_row, k_row, v_row, kt_pages, vt_pages):
    batch = page_table.shape[0]
    row = lambda: pl.BlockSpec((None, 1, MOBA_W), lambda b, pt, sl: (b, 0, 0))
    grid_spec = pltpu.PrefetchScalarGridSpec(
        num_scalar_prefetch=2,
        grid=(batch,),
        in_specs=[row(), row(), row(), pl.BlockSpec(memory_space=pl.ANY), pl.BlockSpec(memory_space=pl.ANY)],
        out_specs=row(),
        scratch_shapes=[pltpu.VMEM((2, HEADS, SEL_PAGES, MOBA_DH, PAGE), F32),
                        pltpu.VMEM((2, HEADS, SEL_PAGES, MOBA_DH, PAGE), F32),
                        pltpu.SemaphoreType.DMA((2, 2))],
    )
    return pl.pallas_call(
        _moba_attend_kernel,
        out_shape=jax.ShapeDtypeStruct((batch, 1, MOBA_W), F32),
        grid_spec=grid_spec,
        compiler_params=pltpu.CompilerParams(dimension_semantics=("arbitrary",), vmem_limit_bytes=VMEM_LIMIT),
        name="moba_attend",
    )(page_table, sel, q_row, k_row, v_row, kt_pages, vt_pages)


def _mla_decode_kernel(ppc, nch, pt_ref, qn_ref, qr_ref, cnew_ref, pnew_ref, wukt_ref, wuv_ref, gkn_ref, gkr_ref,
                       cos_ref, sin_ref, cosn_ref, sinn_ref, ckv_hbm, kpe_hbm, o_ref,
                       cbuf, pbuf, sem, a_ref, m_ref, l_ref, acc_ref):
    b = pl.program_id(0)
    c = pl.program_id(1)
    step = b * nch + c
    slot = step % 2
    lhs_rows = HEADS * NOPE
    tk = ppc * PAGE

    def copies(seq, chunk, slot):
        out = []
        for k in range(ppc):
            page = 0 if seq is None else pt_ref[seq, chunk * ppc + k]
            out.append(pltpu.make_async_copy(ckv_hbm.at[page], cbuf.at[slot, k], sem.at[0, slot]))
            out.append(pltpu.make_async_copy(kpe_hbm.at[page], pbuf.at[slot, k], sem.at[1, slot]))
        return out

    @pl.when(step == 0)
    def _prime():
        for cp in copies(0, 0, 0):
            cp.start()
        a_ref[0:lhs_rows, :] = wukt_ref[...]

    last_chunk = c == nch - 1

    @pl.when(step + 1 < pl.num_programs(0) * nch)
    def _prefetch_next():
        for cp in copies(jnp.where(last_chunk, b + 1, b), jnp.where(last_chunk, 0, c + 1), 1 - slot):
            cp.start()

    @pl.when(c == 0)
    def _start_sequence():
        rowh = lax.broadcasted_iota(jnp.int32, (HEADS, MLA_W), 0)
        colh = lax.broadcasted_iota(jnp.int32, (HEADS, MLA_W), 1)
        qg = jnp.where(colh // NOPE == rowh, qn_ref[...] * gkn_ref[...], 0.0).astype(BF16)
        qlat = _dot(qg, wukt_ref[...])
        a_ref[lhs_rows:lhs_rows + 16, :] = jnp.concatenate([qlat, jnp.zeros_like(qlat)], axis=0).astype(BF16)
        m_ref[...] = jnp.full(m_ref.shape, NEG, F32)
        l_ref[...] = jnp.zeros(l_ref.shape, F32)
        acc_ref[...] = jnp.zeros(acc_ref.shape, F32)

    for cp in copies(None, None, slot):
        cp.wait()

    qr = qr_ref[...].astype(BF16)
    gkr = gkr_ref[...]

    def scores(cb, kpt, cos, sin):
        kt = _dot_nt(a_ref[...], cb)
        n_keys = cb.shape[0]
        kn = kt[0:lhs_rows, :]
        ssn = jnp.sum((kn * kn).reshape(HEADS, NOPE, n_keys), axis=1)
        sn = kt[lhs_rows:lhs_rows + HEADS, :]
        n2 = jnp.sum(kpt * kpt, axis=0, keepdims=True)
        y = kpt * gkr
        y1, y2 = y[0:ROPE // 2, :], y[ROPE // 2:ROPE, :]
        kr = jnp.concatenate([y1 * cos - y2 * sin, y2 * cos + y1 * sin], axis=0)
        sr = _dot(qr, kr.astype(BF16))
        return (sn + sr) * lax.rsqrt((ssn + n2) * (1.0 / QK_DIM) + EPS)

    def accumulate(s, cb):
        m = m_ref[...]
        m2 = jnp.maximum(m, jnp.max(s, axis=1, keepdims=True))
        a = jnp.exp(m - m2)
        p = jnp.exp(s - m2)
        l_ref[...] = a * l_ref[...] + jnp.sum(p, axis=1, keepdims=True)
        acc_ref[...] = a * acc_ref[...] + _dot(p.astype(BF16), cb)
        m_ref[...] = m2

    sub = min(MLA_SCORE_PAGES, ppc)
    cos_c, sin_c = cos_ref[c], sin_ref[c]
    cbs, ss = [], []
    for t in range(ppc // sub):
        cb = cbuf[slot, t * sub:(t + 1) * sub].reshape(sub * PAGE, KV_LORA).astype(BF16)
        kpt = jnp.concatenate([pbuf[slot, k] for k in range(t * sub, (t + 1) * sub)], axis=1)
        lo, hi = t * sub * PAGE, (t + 1) * sub * PAGE
        ss.append(scores(cb, kpt, cos_c[:, lo:hi], sin_c[:, lo:hi]))
        cbs.append(cb)
    accumulate(jnp.concatenate(ss, axis=1), jnp.concatenate(cbs, axis=0))

    @pl.when(last_chunk)
    def _finish_sequence():
        rowi = lax.broadcasted_iota(jnp.int32, (PAGE, KV_LORA), 0)
        lanei = lax.broadcasted_iota(jnp.int32, (ROPE, PAGE), 1)
        cb = jnp.where(rowi == 0, jnp.broadcast_to(cnew_ref[...], (PAGE, KV_LORA)), 0.0).astype(BF16)
        kpt = jnp.where(lanei == 0, jnp.broadcast_to(pnew_ref[...], (ROPE, PAGE)), 0.0)
        key_ok = lax.broadcasted_iota(jnp.int32, (HEADS, PAGE), 1) == 0
        accumulate(jnp.where(key_ok, scores(cb, kpt, cosn_ref[...], sinn_ref[...]), NEG), cb)
        ol = (acc_ref[...] / l_ref[...]).astype(BF16)
        res = _dot(ol, wuv_ref[...])
        rowh = lax.broadcasted_iota(jnp.int32, (HEADS, MLA_W), 0)
        colh = lax.broadcasted_iota(jnp.int32, (HEADS, MLA_W), 1)
        o_ref[...] = jnp.sum(jnp.where(colh // V_DIM == rowh, res, 0.0), axis=0, keepdims=True)


def _mla_decode(page_table, qn, qr, c_new, p_new, wukt, wuv, gkn, gkr_col, cos_t, sin_t, cos_n, sin_n,
                ckv_pages, kpet_pages):
    batch, n_pages = page_table.shape
    ppc = min(MLA_PAGES_PER_STEP, n_pages)
    nch = n_pages // ppc
    seq_spec = lambda shape: pl.BlockSpec((None,) + shape, lambda b, c, pt: (b,) + (0,) * len(shape))
    const = lambda arr: pl.BlockSpec(arr.shape, lambda b, c, pt: (0,) * arr.ndim)
    grid_spec = pltpu.PrefetchScalarGridSpec(
        num_scalar_prefetch=1,
        grid=(batch, nch),
        in_specs=[seq_spec((1, MLA_W)), seq_spec((HEADS, ROPE)), seq_spec((1, KV_LORA)), seq_spec((ROPE, 1)),
                  const(wukt), const(wuv), const(gkn), const(gkr_col), const(cos_t), const(sin_t), const(cos_n),
                  const(sin_n), pl.BlockSpec(memory_space=pl.ANY), pl.BlockSpec(memory_space=pl.ANY)],
        out_specs=seq_spec((1, MLA_W)),
        scratch_shapes=[pltpu.VMEM((2, ppc, PAGE, KV_LORA), F32), pltpu.VMEM((2, ppc, ROPE, PAGE), F32),
                        pltpu.SemaphoreType.DMA((2, 2)),
                        pltpu.VMEM((HEADS * NOPE + 16, KV_LORA), BF16), pltpu.VMEM((HEADS, 1), F32),
                        pltpu.VMEM((HEADS, 1), F32), pltpu.VMEM((HEADS, KV_LORA), F32)],
    )
    return pl.pallas_call(
        functools.partial(_mla_decode_kernel, ppc, nch),
        out_shape=jax.ShapeDtypeStruct((batch, 1, MLA_W), F32),
        grid_spec=grid_spec,
        compiler_params=pltpu.CompilerParams(dimension_semantics=("arbitrary", "arbitrary"),
                                             vmem_limit_bytes=VMEM_LIMIT),
        name="mla_decode",
    )(page_table, qn, qr, c_new, p_new, wukt, wuv, gkn, gkr_col, cos_t, sin_t, cos_n, sin_n, ckv_pages, kpet_pages)


def _rope_angles(pos, half):
    inv = np.power(np.float64(ROPE_THETA), -np.arange(half, dtype=np.float64) / half)
    return np.asarray(pos, np.float64)[:, None] * inv[None, :]


def _row_tables(pos, half):
    ang = _rope_angles(pos, half)
    k = np.arange(LANES) % (2 * half)
    cos = np.cos(ang)[:, k % half]
    sin = np.sin(ang)[:, k % half] * np.where(k < half, -1.0, 1.0).astype(np.float32)[None, :]
    return jnp.asarray(cos, F32), jnp.asarray(sin, F32)


def _block_ones(rows_per_group, groups, cols):
    m = np.zeros((groups * rows_per_group, cols), np.float32)
    m[np.arange(groups * rows_per_group), np.arange(groups * rows_per_group) // rows_per_group] = 1.0
    return m


def kernel(x_prompt, x_sample, cache_moba_k, cache_moba_v, cache_mla_ckv, cache_mla_kpe, page_table, norm_gain, w_in, moba_q_gain, moba_k_gain, mla_q_lat_gain, w_uq, mla_q_gain, mla_kv_lat_gain, w_uk, w_uv, mla_k_gain, w_branch_a, w_branch_b, w_out):
    batch, seq, _ = x_prompt.shape
    dec_batch, dec_seq, _ = x_sample.shape
    depth = norm_gain.shape[0]
    assert depth == 1 and dec_seq == 1
    n_pages = page_table.shape[1]
    past = n_pages * PAGE
    assert seq % ATTN_TILE == 0 and seq // MOBA_BLOCK <= 8 and past % MOBA_BLOCK == 0
    assert past // MOBA_BLOCK >= MOBA_TOPK
    assert n_pages % min(PAGES_PER_STEP, n_pages) == 0 and n_pages % min(MLA_PAGES_PER_STEP, n_pages) == 0

    w = w_in[0]
    bf = lambda a: a.astype(BF16)
    o = 4 * MOBA_W
    wqk, wv, wg = bf(w[:, 0:2 * MOBA_W]), bf(w[:, 2 * MOBA_W:3 * MOBA_W]), bf(w[:, 3 * MOBA_W:o])
    w2 = bf(jnp.pad(w[:, o:o + Q_LORA + KV_LORA + ROPE], ((0, 0), (0, LANES - ROPE))))
    o += Q_LORA + KV_LORA + ROPE
    w3 = bf(w[:, o:o + MLA_W + 2 * D_MODEL])
    wuqn = bf(w_uq[0][:, :, :NOPE].reshape(Q_LORA, HEADS * NOPE))
    wuqr = bf(w_uq[0][:, :, NOPE:].reshape(Q_LORA, HEADS * ROPE))
    wuk = bf(w_uk[0].reshape(KV_LORA, HEADS * NOPE))
    wuv = bf(w_uv[0].reshape(KV_LORA, MLA_W))
    tile8 = lambda g: jnp.tile(g, HEADS)[None, :].astype(F32)
    gains = [tile8(moba_q_gain[0]), tile8(moba_k_gain[0]), mla_q_lat_gain[0][None, :], mla_kv_lat_gain[0][None, :],
             tile8(mla_q_gain[0][:NOPE]), tile8(mla_q_gain[0][NOPE:]), tile8(mla_k_gain[0][:NOPE]),
             tile8(mla_k_gain[0][NOPE:])]
    place = lambda width, off: np.eye(HEADS * LANES, dtype=np.float32)[
        (np.arange(HEADS * width) // width) * LANES + off + np.arange(HEADS * width) % width]
    helpers = [
        jnp.asarray(_block_ones(MOBA_DH, HEADS, HEADS) @ _block_ones(MOBA_DH, HEADS, HEADS).T, BF16),
        jnp.asarray(_block_ones(NOPE, HEADS, LANES), BF16),
        jnp.asarray(_block_ones(ROPE, HEADS, LANES), BF16),
        jnp.asarray(_block_ones(NOPE, HEADS, LANES).T, BF16),
        jnp.asarray(_block_ones(ROPE, HEADS, LANES).T, BF16),
        jnp.asarray(np.concatenate([np.tile(np.eye(ROPE, dtype=np.float32), (1, HEADS)),
                                    np.zeros((LANES - ROPE, HEADS * ROPE), np.float32)], axis=0), BF16),
        jnp.asarray(place(NOPE, 0), BF16),
        jnp.asarray(place(ROPE, NOPE), BF16),
    ]
    consts_p = [norm_gain[0][None, :], wqk, wv.T, wg, w2, w3, wuqn, wuqr, wuk, wuv.T] + gains + helpers
    consts_s = [norm_gain[0][None, :], wqk, wv, wg, w2, w3, wuqn, wuqr, wuk, wuv] + gains + helpers
    assert len(consts_p) - len(helpers) == N_FRONT_WEIGHTS
    pos_p = np.arange(seq)
    pos_s = np.full((dec_batch,), past)
    tables_p = _row_tables(pos_p, MOBA_DH // 2) + _row_tables(pos_p, ROPE // 2)
    tables_s = _row_tables(pos_s, MOBA_DH // 2) + _row_tables(pos_s, ROPE // 2)
    wa = bf(w_branch_a[0])
    wb = bf(w_branch_b[0])
    wo = bf(w_out[0])

    xp = x_prompt.reshape(batch * seq, D_MODEL)
    (qa, kat, vat, ga, ckv, kpe, gb, g1, g2, qcat, kcat, vmt) = _front(xp, consts_p, tables_p, True, seq)
    xs = x_sample.reshape(dec_batch, D_MODEL)
    (qa_s, ka_s, va_s, ga_s, ckv_s, kpe_s, gb_s, g1_s, g2_s, qn_s, qr_s) = _front(xs, consts_s, tables_s, False, 1)
    kt_pages = jnp.transpose(cache_moba_k[0], (0, 2, 3, 1))
    vt_pages = jnp.transpose(cache_moba_v[0], (0, 2, 3, 1))
    ckv_pages = cache_mla_ckv[0]
    kpet_pages = jnp.transpose(cache_mla_kpe[0], (0, 2, 1))

    sel_ppc = min(PAGES_PER_STEP, n_pages)
    sel_nch = n_pages // sel_ppc
    nq = seq // ATTN_TILE
    loop_iterations = batch * nq * (nq + 1) // 2
    chunks = dec_batch * sel_nch
    chunks_a = min(chunks, (loop_iterations // sel_nch) * sel_nch)
    assert chunks - chunks_a <= loop_iterations, "paged decode too large to hide under the prompt attention"
    oa, sel_a = _moba_prompt(qa, kat, vat, ga, batch, seq, page_table, qa_s, kt_pages, (0, chunks_a, sel_ppc, sel_nch))
    ob, sel_b = _mla_prompt(qcat, kcat, vmt, gb, batch, seq, page_table, qa_s, kt_pages,
                            (chunks_a, chunks - chunks_a, sel_ppc, sel_nch))
    y_prompt = _merge(xp, oa, ob, (), g1, g2, wa, wb, wo).reshape(batch, seq, D_MODEL)

    row3 = lambda a: a.reshape(dec_batch, 1, MOBA_W)
    first_b = (jnp.arange(dec_batch) >= chunks_a // sel_nch)[:, None, None]
    sel = jnp.where(first_b, sel_b[:dec_batch], sel_a[:dec_batch])[:, :, :MOBA_TOPK].reshape(dec_batch, HEADS * MOBA_TOPK)
    oa_s = _moba_attend(page_table, sel, row3(qa_s), row3(ka_s), row3(va_s), kt_pages, vt_pages).reshape(dec_batch, MOBA_W)

    ppc = min(MLA_PAGES_PER_STEP, n_pages)
    ang_past = _rope_angles(np.arange(past), ROPE // 2).T
    ang_past = ang_past.reshape(ROPE // 2, n_pages // ppc, ppc * PAGE).transpose(1, 0, 2)
    ang_new = np.broadcast_to(_rope_angles(np.array([past]), ROPE // 2).T, (ROPE // 2, PAGE))
    wukt = jnp.transpose(w_uk[0].reshape(KV_LORA, HEADS * NOPE)).astype(BF16)
    ob_s = _mla_decode(
        page_table, qn_s.reshape(dec_batch, 1, MLA_W), qr_s.reshape(dec_batch, HEADS, ROPE),
        ckv_s.reshape(dec_batch, 1, KV_LORA), kpe_s.reshape(dec_batch, ROPE, 1),
        wukt, wuv, tile8(mla_k_gain[0][:NOPE]), mla_k_gain[0][NOPE:].reshape(ROPE, 1),
        jnp.asarray(np.cos(ang_past), F32), jnp.asarray(np.sin(ang_past), F32),
        jnp.asarray(np.cos(ang_new), F32), jnp.asarray(np.sin(ang_new), F32),
        ckv_pages, kpet_pages).reshape(dec_batch, MLA_W)
    y_sample = _merge(xs, oa_s, ob_s, (ga_s, gb_s), g1_s, g2_s, wa, wb, wo).reshape(dec_batch, 1, D_MODEL)

    p5 = lambda a_t: a_t.reshape(1, batch, HEADS, MOBA_DH, seq).transpose(0, 1, 4, 2, 3)
    s5 = lambda a, w: a.reshape(1, dec_batch, 1, HEADS, w)
    return (y_prompt, y_sample,
            p5(kat), p5(vat), ckv.reshape(1, batch, seq, KV_LORA), kpe.reshape(1, batch, seq, ROPE),
            s5(ka_s, MOBA_DH), s5(va_s, MOBA_DH), ckv_s.reshape(1, dec_batch, 1, KV_LORA),
            kpe_s.reshape(1, dec_batch, 1, ROPE))
```
